```python
import math
import jax, jax.numpy as jnp
from jax import lax
import numpy as np

D_MODEL = 1024
BATCH = 32
SEQ = 256
DEPTH = 2
DEC_BATCH = 8
DEC_SEQ = 4096
PAST_LEN = 512

F32 = jnp.float32
GRID_W = 64
HA = 4
DQK = 32
DVA = 2 * DQK
WA = HA * DVA
ROPE_BASE = 10000.0
Q_BLOCK = 128
HB = 8
PB = 64
WB = HB * PB
GB = 2
NB = 64
CONV_W = 3
CONV_CH = WB + 2 * GB * NB
SSD_CHUNK = 128
HC = 4
DKC = 64
DVC = 64
WC = HC * DVC
MLSTM_CHUNK = 128
D_MIX = WA + WB + WC
IN_SIZES = (WA, WA, WA, WB, CONV_CH, 2 * HB, WC, WC, WC, WC, 4 * HC)
N_IN = sum(IN_SIZES)
IN_SPLITS = tuple(int(s) for s in np.cumsum(IN_SIZES)[:-1])
N_EXPERTS = 32
TOP_K = 4
D_FF = D_MODEL
SWIGLU_LIMIT = 7.0
SWIGLU_ALPHA = 1.702
MOE_BLOCK = 128
ALPHA = (2 * DEPTH) ** 0.25
BETA = (8 * DEPTH) ** -0.25
LN_EPS = 1e-5

kernel_name = 'hybrid_diffattn_ssd_mlstm_moe_denoise_step'


def layer_norm(x, g, b):
    xf = x.astype(F32)
    mu = jnp.mean(xf, axis=-1, keepdims=True)
    var = jnp.mean(jnp.square(xf - mu), axis=-1, keepdims=True)
    return ((xf - mu) * lax.rsqrt(var + LN_EPS) * g + b).astype(x.dtype)


def rms_norm(x, g):
    xf = x.astype(F32)
    return xf * lax.rsqrt(jnp.mean(jnp.square(xf), axis=-1, keepdims=True) + LN_EPS) * g


def group_rms_norm(x, groups, g):
    shp = x.shape
    xg = x.reshape(shp[:-1] + (groups, shp[-1] // groups))
    return rms_norm(xg, g.reshape(groups, shp[-1] // groups)).reshape(shp)


def head_layer_norm(x, g):
    xf = x.astype(F32)
    mu = jnp.mean(xf, axis=-1, keepdims=True)
    var = jnp.mean(jnp.square(xf - mu), axis=-1, keepdims=True)
    return (xf - mu) * lax.rsqrt(var + LN_EPS) * g.reshape(x.shape[-2:])


def axial_rope(x):
    Bsz, L = x.shape[:2]
    rows = L // GRID_W
    row = jnp.repeat(jnp.arange(rows), GRID_W)
    col = jnp.tile(jnp.arange(GRID_W), rows)
    xr = x.reshape(Bsz, L, -1, DQK).astype(F32)
    half = DQK // 2
    quarter = half // 2
    inv = ROPE_BASE ** (-jnp.arange(quarter, dtype=F32) / quarter)

    def rot(xa, pos):
        ang = pos.astype(F32)[:, None] * inv[None, :]
        cos = jnp.cos(ang)[None, :, None, :]
        sin = jnp.sin(ang)[None, :, None, :]
        x1, x2 = xa[..., :quarter], xa[..., quarter:]
        return jnp.concatenate([x1 * cos - x2 * sin, x2 * cos + x1 * sin], axis=-1)

    out = jnp.concatenate([rot(xr[..., :half], row), rot(xr[..., half:], col)], axis=-1)
    return out.reshape(x.shape).astype(x.dtype)


def diff_attention(q, k, v, lam):
    Bsz, Lq = q.shape[:2]
    nb = Lq // Q_BLOCK
    qb = jnp.moveaxis(q.reshape(Bsz, nb, Q_BLOCK, HA, 2, DQK), 1, 0)
    scale = DQK ** -0.5

    def block(qi):
        s = jnp.einsum('bqhmd,bkhmd->bhmqk', qi, k).astype(F32) * scale
        pr = jax.nn.softmax(s, axis=-1)
        w = pr[:, :, 0] - lam * pr[:, :, 1]
        return jnp.einsum('bhqk,bkhd->bqhd', w.astype(v.dtype), v)

    o = lax.map(block, qb)
    return jnp.moveaxis(o, 0, 1).reshape(Bsz, Lq, HA, DVA)


def centred_dwconv(x, w, b):
    ch = x.shape[-1]
    pad = CONV_W // 2
    y = lax.conv_general_dilated(x, w[:, None, :].astype(x.dtype), window_strides=(1,),
                                 padding=[(pad, pad)], dimension_numbers=('NWC', 'WIO', 'NWC'),
                                 feature_group_count=ch)
    return y + b.astype(x.dtype)


def ssd_scan(xdt, a, bm, cm, h0):
    Bsz, L = xdt.shape[:2]
    nc = L // SSD_CHUNK
    R = HB // GB
    xc = xdt.reshape(Bsz, nc, SSD_CHUNK, GB, R, PB)
    ac = a.reshape(Bsz, nc, SSD_CHUNK, GB, R)
    bc = bm.reshape(Bsz, nc, SSD_CHUNK, GB, NB)
    cc = cm.reshape(Bsz, nc, SSD_CHUNK, GB, NB)
    acum = jnp.cumsum(ac, axis=2)
    seg = acum[:, :, :, None] - acum[:, :, None, :]
    causal = jnp.tril(jnp.ones((SSD_CHUNK, SSD_CHUNK), bool))[None, None, :, :, None, None]
    decay = jnp.exp(jnp.where(causal, seg, -jnp.inf))
    cb = jnp.einsum('bcign,bcjgn->bcijg', cc, bc)
    y_diag = jnp.einsum('bcijgr,bcjgrp->bcigrp', cb[..., None] * decay, xc)
    to_end = jnp.exp(acum[:, :, -1:] - acum)
    st_local = jnp.einsum('bcjgn,bcjgrp->bcgrpn', bc, xc * to_end[..., None])
    chunk_decay = jnp.exp(acum[:, :, -1])

    def step(h, inp):
        s, dcy = inp
        return h * dcy[..., None, None] + s, h

    hfin, h_start = lax.scan(step, h0.reshape(Bsz, GB, R, PB, NB),
                             (jnp.moveaxis(st_local, 1, 0), jnp.moveaxis(chunk_decay, 1, 0)))
    h_start = jnp.moveaxis(h_start, 0, 1)
    y_off = jnp.einsum('bcign,bcgrpn->bcigrp', cc, h_start) * jnp.exp(acum)[..., None]
    y = (y_diag + y_off).reshape(Bsz, L, HB, PB)
    return y, hfin.reshape(Bsz, HB, PB, NB)


def mlstm_scan(q, k, v, ig, lf, c0, n0, m0):
    Bsz, L = q.shape[:2]
    T = MLSTM_CHUNK
    nc = L // T
    qc = q.reshape(Bsz, nc, T, HC, DKC)
    kc = k.reshape(Bsz, nc, T, HC, DKC)
    vc = v.reshape(Bsz, nc, T, HC, DVC)
    igc = ig.reshape(Bsz, nc, T, HC)
    b = jnp.cumsum(lf.reshape(Bsz, nc, T, HC), axis=2)
    causal = jnp.tril(jnp.ones((T, T), bool))[None, None, :, :, None]
    dmat = jnp.where(causal, b[:, :, :, None] - b[:, :, None, :] + igc[:, :, None, :], -jnp.inf)
    g_end = b[:, :, -1:] - b + igc
    m_loc = jnp.max(g_end, axis=2)
    w_end = jnp.exp(g_end - m_loc[:, :, None])
    c_loc = jnp.einsum('bcshk,bcshv->bchkv', kc * w_end[..., None], vc)
    n_loc = jnp.einsum('bcsh,bcshk->bchk', w_end, kc)
    b_end = b[:, :, -1]

    def step(carry, inp):
        cm_, nm_, mm_ = carry
        cl, nl, ml, be = inp
        m_new = jnp.maximum(be + mm_, ml)
        a_prev = jnp.exp(be + mm_ - m_new)
        a_loc = jnp.exp(ml - m_new)
        c_new = a_prev[..., None, None] * cm_ + a_loc[..., None, None] * cl
        n_new = a_prev[..., None] * nm_ + a_loc[..., None] * nl
        return (c_new, n_new, m_new), (cm_, nm_, mm_)

    (cf, nf, mf), (cs, ns, ms) = lax.scan(
        step, (c0, n0, m0),
        (jnp.moveaxis(c_loc, 1, 0), jnp.moveaxis(n_loc, 1, 0), jnp.moveaxis(m_loc, 1, 0), jnp.moveaxis(b_end, 1, 0)))
    cs = jnp.moveaxis(cs, 0, 1)
    ns = jnp.moveaxis(ns, 0, 1)
    ms = jnp.moveaxis(ms, 0, 1)
    inter = b + ms[:, :, None, :]
    m_t = jnp.maximum(inter, jnp.max(dmat, axis=3))
    w_intra = jnp.exp(dmat - m_t[:, :, :, None, :])
    w_inter = jnp.exp(inter - m_t)
    qk = jnp.einsum('bcthd,bcshd->bctsh', qc, kc) * w_intra
    num = jnp.einsum('bctsh,bcshv->bcthv', qk, vc) + w_inter[..., None] * jnp.einsum('bcthk,bchkv->bcthv', qc, cs)
    den = jnp.sum(qk, axis=3) + w_inter * jnp.einsum('bcthk,bchk->bcth', qc, ns)
    h = num / jnp.maximum(jnp.abs(den), jnp.exp(-m_t))[..., None]
    return h.reshape(Bsz, L, HC, DVC), (cf, nf, mf)


def attn_branch(qa, ka, va, p, layer, ctx):
    Bsz, L = qa.shape[:2]
    lam_init = 0.8 - 0.6 * math.exp(-0.3 * layer)
    lam = (jnp.exp(jnp.sum(p['lam_q1'] * p['lam_k1'])) - jnp.exp(jnp.sum(p['lam_q2'] * p['lam_k2']))
           + lam_init).astype(F32)
    q = qa.reshape(Bsz, L, HA, 2, DQK)
    k = ka.reshape(Bsz, L, HA, 2, DQK)
    v = va.reshape(Bsz, L, HA, DVA)
    if ctx is None:
        keys, vals = k, v
    else:
        q = axial_rope(q)
        ck = ctx['k'].reshape(Bsz, -1, HA, 2, DQK).astype(k.dtype)
        keys = jnp.concatenate([axial_rope(k), ck], axis=1)
        vals = jnp.concatenate([v, ctx['v'].astype(v.dtype)], axis=1)
    o = diff_attention(q, keys, vals, lam)
    o = rms_norm(o, p['attn_g']) * (1.0 - lam_init)
    return o.reshape(Bsz, L, WA), (k.reshape(Bsz, L, HA, 2 * DQK), v)


def ssd_branch(z, xbc, dt_raw, p, ctx):
    Bsz, L = z.shape[:2]
    xbc = jax.nn.silu(centred_dwconv(xbc, p['conv_w'], p['conv_b'])).astype(F32)
    xs = xbc[..., :WB].reshape(Bsz, L, HB, PB)
    bm = xbc[..., WB:WB + GB * NB].reshape(Bsz, L, GB, NB)
    cm = xbc[..., WB + GB * NB:].reshape(Bsz, L, GB, NB)
    dt = jax.nn.softplus(dt_raw.astype(F32).reshape(Bsz, L, 2, HB) + p['dt_bias'].astype(F32))
    a_neg = -jnp.exp(p['a_log'].astype(F32))
    y = xs * p['d_skip'].astype(F32)[:, None]
    finals = []
    for d in range(2):
        xin = xs * dt[:, :, d, :, None]
        ain = dt[:, :, d] * a_neg[d]
        bin_, cin = bm, cm
        h0 = jnp.zeros((Bsz, HB, PB, NB), F32) if ctx is None else ctx['ssd'][:, d].astype(F32)
        if d == 1:
            xin, ain, bin_, cin = (jnp.flip(t, axis=1) for t in (xin, ain, bin_, cin))
        yd, hf = ssd_scan(xin, ain, bin_, cin, h0)
        y = y + (jnp.flip(yd, axis=1) if d == 1 else yd)
        finals.append(hf)
    y = y.reshape(Bsz, L, WB) * jax.nn.silu(z.astype(F32))
    return group_rms_norm(y, GB, p['ssd_g']), jnp.stack(finals, axis=1)


def mlstm_branch(qm, km, vm, om, gm, p, ctx):
    Bsz, L = qm.shape[:2]
    q = qm.astype(F32).reshape(Bsz, L, HC, DKC)
    k = km.astype(F32).reshape(Bsz, L, HC, DKC) * (DKC ** -0.5)
    v = vm.astype(F32).reshape(Bsz, L, HC, DVC)
    g = gm.astype(F32).reshape(Bsz, L, 2, 2, HC)
    hsum = jnp.zeros((Bsz, L, HC, DVC), F32)
    fin_c, fin_n, fin_m = [], [], []
    for d in range(2):
        ig = g[:, :, d, 0] + p['ig_b'][d]
        lf = jax.nn.log_sigmoid(g[:, :, d, 1] + p['fg_b'][d])
        if ctx is None:
            c0 = jnp.zeros((Bsz, HC, DKC, DVC), F32)
            n0 = jnp.zeros((Bsz, HC, DKC), F32)
            m0 = jnp.zeros((Bsz, HC), F32)
        else:
            c0 = ctx['C'][:, d].astype(F32)
            n0 = ctx['n'][:, d].astype(F32)
            m0 = ctx['m'][:, d].astype(F32)
        seqs = (q, k, v, ig, lf)
        if d == 1:
            seqs = tuple(jnp.flip(t, axis=1) for t in seqs)
        hd, (cf, nf, mf) = mlstm_scan(*seqs, c0, n0, m0)
        hsum = hsum + (jnp.flip(hd, axis=1) if d == 1 else hd)
        fin_c.append(cf)
        fin_n.append(nf)
        fin_m.append(mf)
    o = jax.nn.sigmoid(om.astype(F32)).reshape(Bsz, L, HC, DVC)
    y = o * head_layer_norm(hsum, p['mlstm_g'])
    return y.reshape(Bsz, L, WC), (jnp.stack(fin_c, 1), jnp.stack(fin_n, 1), jnp.stack(fin_m, 1))


def mixer(h, p, layer, ctx):
    proj = h @ p['w_in']
    qa, ka, va, z, xbc, dt_raw, qm, km, vm, om, gm = jnp.split(proj, IN_SPLITS, axis=-1)
    oa, (ck, cv) = attn_branch(qa, ka, va, p, layer, ctx)
    ob, s_ssd = ssd_branch(z, xbc, dt_raw, p, ctx)
    oc, (s_c, s_n, s_m) = mlstm_branch(qm, km, vm, om, gm, p, ctx)
    out = jnp.concatenate([oa, ob, oc], axis=-1).astype(h.dtype) @ p['w_out']
    new_ctx = (ck, cv, s_ssd, s_c, s_n, s_m) if ctx is None else None
    return out, new_ctx


def moe_ffn(h, router_w, router_b, w_gu, b_gu, w_dn, b_dn):
    Bsz, L, D = h.shape
    x = h.reshape(-1, D)
    T = x.shape[0]
    logits = (x @ router_w + router_b).astype(F32)
    top_v, top_i = lax.top_k(logits, TOP_K)
    gate = jax.nn.softmax(top_v, axis=-1)
    flat_e = top_i.reshape(-1)
    order = jnp.argsort(flat_e)
    e_sorted = flat_e[order]
    counts = jnp.bincount(flat_e, length=N_EXPERTS)
    padded = (counts + MOE_BLOCK - 1) // MOE_BLOCK * MOE_BLOCK
    pad_end = jnp.cumsum(padded)
    pad_start = pad_end - padded
    start = jnp.cumsum(counts) - counts
    dest = pad_start[e_sorted] + jnp.arange(T * TOP_K) - start[e_sorted]
    n_blocks = -(-(T * TOP_K) // MOE_BLOCK) + N_EXPERTS
    n_rows = n_blocks * MOE_BLOCK
    row_tok = jnp.full((n_rows,), T, jnp.int32).at[dest].set((order // TOP_K).astype(jnp.int32))
    row_gate = jnp.zeros((n_rows,), F32).at[dest].set(gate.reshape(-1)[order])
    blk_expert = jnp.minimum(jnp.searchsorted(pad_end, jnp.arange(n_blocks) * MOE_BLOCK, side='right'),
                             N_EXPERTS - 1)
    x_pad = jnp.concatenate([x, jnp.zeros((1, D), x.dtype)], axis=0)
    xb = x_pad[row_tok].reshape(n_blocks, MOE_BLOCK, D)

    def expert_block(args):
        xi, e = args
        gu = xi @ w_gu[e] + b_gu[e]
        g = jnp.minimum(gu[:, :D_FF], SWIGLU_LIMIT)
        u = jnp.clip(gu[:, D_FF:], -SWIGLU_LIMIT, SWIGLU_LIMIT)
        act = g * jax.nn.sigmoid(SWIGLU_ALPHA * g) * (u + 1.0)
        return act @ w_dn[e] + b_dn[e]

    yb = lax.map(expert_block, (xb, blk_expert)).reshape(n_rows, D)
    y = jnp.zeros((T + 1, D), yb.dtype).at[row_tok].add(yb * row_gate[:, None].astype(yb.dtype))
    return y[:T].reshape(Bsz, L, D)


def trunk_layer(x, cond, p, layer, ctx):
    mod = (jax.nn.silu(cond) @ p['w_mod'] + p['b_mod'])[:, None, :].astype(x.dtype)
    sh1, sc1, g1, sh2, sc2, g2 = jnp.split(mod, 6, axis=-1)
    mix, new_ctx = mixer(x * (1 + sc1) + sh1, p, layer, ctx)
    x = layer_norm(ALPHA * x + g1 * mix, p['ln1_g'], p['ln1_b'])
    ff = moe_ffn(x * (1 + sc2) + sh2, p['router_w'], p['router_b'], p['w_gu'], p['b_gu'], p['w_dn'], p['b_dn'])
    x = layer_norm(ALPHA * x + g2 * ff, p['ln2_g'], p['ln2_b'])
    return x, new_ctx


def setup_inputs(seed: int = 0) -> dict:
    key = jax.random.key(seed)
    ks = iter(jax.random.split(key, 64))

    def nrm(shape, scale=1.0):
        return scale * jax.random.normal(next(ks), shape, F32)

    dt0 = jnp.exp(jax.random.uniform(next(ks), (DEPTH, 2, HB), F32, math.log(1e-3), math.log(1e-1)))
    a_log = jnp.log(jax.random.uniform(next(ks), (DEPTH, 2, HB), F32, 1.0, 16.0))
    return {
        'x_prompt': nrm((BATCH, SEQ, D_MODEL)),
        'x_sample': nrm((DEC_BATCH, DEC_SEQ, D_MODEL)),
        'c': nrm((DEC_BATCH, D_MODEL)),
        'cache_attn_k': nrm((DEC_BATCH, DEPTH, PAST_LEN, HA, 2 * DQK)),
        'cache_attn_v': nrm((DEC_BATCH, DEPTH, PAST_LEN, HA, DVA)),
        'state_ssd': nrm((DEC_BATCH, DEPTH, 2, HB, PB, NB), 0.1),
        'state_mlstm_C': nrm((DEC_BATCH, DEPTH, 2, HC, DKC, DVC), 0.1),
        'state_mlstm_n': nrm((DEC_BATCH, DEPTH, 2, HC, DKC), 0.5),
        'state_mlstm_m': nrm((DEC_BATCH, DEPTH, 2, HC)),
        'c_ctx': nrm((D_MODEL,)),
        'w_mod': nrm((DEPTH, D_MODEL, 6 * D_MODEL), 0.5 * D_MODEL ** -0.5),
        'b_mod': nrm((DEPTH, 6 * D_MODEL), 0.02),
        'w_in': nrm((DEPTH, D_MODEL, N_IN), D_MODEL ** -0.5),
        'lam_q1': nrm((DEPTH, DQK), 0.1),
        'lam_k1': nrm((DEPTH, DQK), 0.1),
        'lam_q2': nrm((DEPTH, DQK), 0.1),
        'lam_k2': nrm((DEPTH, DQK), 0.1),
        'attn_g': 1.0 + nrm((DEPTH, DVA), 0.02),
        'conv_w': nrm((DEPTH, CONV_W, CONV_CH), CONV_W ** -0.5),
        'conv_b': nrm((DEPTH, CONV_CH), 0.02),
        'dt_bias': dt0 + jnp.log(-jnp.expm1(-dt0)),
        'a_log': a_log,
        'd_skip': 1.0 + nrm((DEPTH, HB), 0.02),
        'ssd_g': 1.0 + nrm((DEPTH, WB), 0.02),
        'ig_b': nrm((DEPTH, 2, HC), 0.1),
        'fg_b': jnp.linspace(3.0, 6.0, HC, dtype=F32) + nrm((DEPTH, 2, HC), 0.1),
        'mlstm_g': 1.0 + nrm((DEPTH, WC), 0.02),
        'w_out': nrm((DEPTH, D_MIX, D_MODEL), BETA * D_MIX ** -0.5),
        'ln1_g': 1.0 + nrm((DEPTH, D_MODEL), 0.02),
        'ln1_b': nrm((DEPTH, D_MODEL), 0.02),
        'router_w': nrm((DEPTH, D_MODEL, N_EXPERTS), D_MODEL ** -0.5),
        'router_b': nrm((DEPTH, N_EXPERTS), 0.01),
        'w_gu': nrm((DEPTH, N_EXPERTS, D_MODEL, 2 * D_FF), D_MODEL ** -0.5),
        'b_gu': nrm((DEPTH, N_EXPERTS, 2 * D_FF), 0.02),
        'w_dn': nrm((DEPTH, N_EXPERTS, D_FF, D_MODEL), BETA * D_FF ** -0.5),
        'b_dn': nrm((DEPTH, N_EXPERTS, D_MODEL), 0.02),
        'ln2_g': 1.0 + nrm((DEPTH, D_MODEL), 0.02),
        'ln2_b': nrm((DEPTH, D_MODEL), 0.02),
    }


def reference(x_prompt, x_sample, c, cache_attn_k, cache_attn_v, state_ssd, state_mlstm_C, state_mlstm_n,
              state_mlstm_m, c_ctx, w_mod, b_mod, w_in, lam_q1, lam_k1, lam_q2, lam_k2, attn_g, conv_w, conv_b,
              dt_bias, a_log, d_skip, ssd_g, ig_b, fg_b, mlstm_g, w_out, ln1_g, ln1_b, router_w, router_b,
              w_gu, b_gu, w_dn, b_dn, ln2_g, ln2_b):
    y_prompt = x_prompt
    y_sample = x_sample
    cond_ctx = c_ctx[None, :]
    ks_, vs_, ss_, cs_, ns_, ms_ = [], [], [], [], [], []
    for l in range(DEPTH):
        p = {'w_mod': w_mod[l], 'b_mod': b_mod[l], 'w_in': w_in[l],
             'lam_q1': lam_q1[l], 'lam_k1': lam_k1[l], 'lam_q2': lam_q2[l], 'lam_k2': lam_k2[l],
             'attn_g': attn_g[l], 'conv_w': conv_w[l], 'conv_b': conv_b[l], 'dt_bias': dt_bias[l],
             'a_log': a_log[l], 'd_skip': d_skip[l], 'ssd_g': ssd_g[l], 'ig_b': ig_b[l], 'fg_b': fg_b[l],
             'mlstm_g': mlstm_g[l], 'w_out': w_out[l], 'ln1_g': ln1_g[l], 'ln1_b': ln1_b[l],
             'router_w': router_w[l], 'router_b': router_b[l], 'w_gu': w_gu[l], 'b_gu': b_gu[l],
             'w_dn': w_dn[l], 'b_dn': b_dn[l], 'ln2_g': ln2_g[l], 'ln2_b': ln2_b[l]}
        y_prompt, st = trunk_layer(y_prompt, cond_ctx, p, l, None)
        ks_.append(st[0].astype(x_prompt.dtype))
        vs_.append(st[1].astype(x_prompt.dtype))
        ss_.append(st[2].astype(x_prompt.dtype))
        cs_.append(st[3].astype(x_prompt.dtype))
        ns_.append(st[4].astype(x_prompt.dtype))
        ms_.append(st[5].astype(x_prompt.dtype))
        ctx = {'k': cache_attn_k[:, l], 'v': cache_attn_v[:, l], 'ssd': state_ssd[:, l],
               'C': state_mlstm_C[:, l], 'n': state_mlstm_n[:, l], 'm': state_mlstm_m[:, l]}
        y_sample, _ = trunk_layer(y_sample, c, p, l, ctx)
    new_attn_k = jnp.stack(ks_, axis=1)
    new_attn_v = jnp.stack(vs_, axis=1)
    new_ssd = jnp.stack(ss_, axis=1)
    new_mlstm_C = jnp.stack(cs_, axis=1)
    new_mlstm_n = jnp.stack(ns_, axis=1)
    new_mlstm_m = jnp.stack(ms_, axis=1)
    return (y_prompt, y_sample, new_attn_k, new_attn_v, new_ssd, new_mlstm_C, new_mlstm_n, new_mlstm_m)
```

```python
import functools
import math

import jax
import jax.numpy as jnp
import numpy as np
from jax import lax
from jax.experimental import pallas as pl
from jax.experimental.pallas import tpu as pltpu

F32 = jnp.float32
BF16 = jnp.bfloat16
HIGHEST = lax.Precision.HIGHEST

D_MODEL = 1024
DEPTH = 2
GRID_W = 64
HA = 4
DQK = 32
DVA = 2 * DQK
WA = HA * DVA
ROPE_BASE = 10000.0
HB = 8
PB = 64
WB = HB * PB
GB = 2
NB = 64
CONV_W = 3
CONV_CH = WB + 2 * GB * NB
HC = 4
DKC = 64
DVC = 64
WC = HC * DVC
D_MIX = WA + WB + WC
CHUNK = 128
N_EXPERTS = 32
TOP_K = 4
D_FF = D_MODEL
SWIGLU_LIMIT = 7.0
SWIGLU_ALPHA = 1.702
ALPHA = (2 * DEPTH) ** 0.25
LN_EPS = 1e-5

C_QA, C_KA, C_VA, C_OM = 0, 256, 512, 768
C_Z = 1024
C_XBC = 1536
C_QM, C_KM, C_VM = 2304, 2560, 2816
C_DTGM = 3072
N_PROJ = 3200
DT_LANES = 2 * HB
GM_LANE0 = DT_LANES

VMEM_LIMIT = 56 * 1024 * 1024
LANES = 128
SUBLANES = 8


def _cparams(*sem):
    return pltpu.CompilerParams(dimension_semantics=sem, vmem_limit_bytes=VMEM_LIMIT)


def _sigmoid(x):
    return 1.0 / (1.0 + jnp.exp(-x))


def _softplus(x):
    return jnp.maximum(x, 0.0) + jnp.log1p(jnp.exp(-jnp.abs(x)))


def _mod_kernel(c_ref, w_ref, b_ref, o_ref):
    c = c_ref[...]
    s = (c * _sigmoid(c)).astype(BF16)
    o_ref[...] = jnp.dot(s, w_ref[...].astype(BF16), preferred_element_type=F32) + b_ref[...]


def _modulation(cond, w_mod, b_mod):
    rows = cond.shape[0]
    n = w_mod.shape[1]
    tn = D_MODEL
    return pl.pallas_call(
        _mod_kernel,
        grid=(n // tn,),
        in_specs=[pl.BlockSpec((rows, D_MODEL), lambda j: (0, 0)),
                  pl.BlockSpec((D_MODEL, tn), lambda j: (0, j)),
                  pl.BlockSpec((1, tn), lambda j: (0, j))],
        out_specs=pl.BlockSpec((rows, tn), lambda j: (0, j)),
        out_shape=jax.ShapeDtypeStruct((rows, n), F32),
        compiler_params=_cparams("arbitrary"),
        name="modulation",
    )(cond, w_mod, b_mod.reshape(1, n))


def _inproj_kernel(x_ref, mod_ref, w_ref, cos_ref, sa_ref, sb_ref, o_ref, *, rope):
    x = x_ref[0]
    sh = mod_ref[0, 0:1, :]
    sc = mod_ref[0, 1:2, :]
    h = (x * (1.0 + sc) + sh).astype(BF16)
    p = jnp.dot(h, w_ref[...], preferred_element_type=F32)

    def rot(t):
        return (t * cos_ref[...] + pltpu.roll(t, WA - DQK // 4, 1) * sa_ref[...]
                + pltpu.roll(t, DQK // 4, 1) * sb_ref[...])

    q = p[:, C_QA:C_QA + WA]
    k = p[:, C_KA:C_KA + WA]
    if rope:
        q = rot(q)
        k = rot(k)
    o_ref[0, :, C_QA:C_QA + WA] = q * (DQK ** -0.5)
    o_ref[0, :, C_KA:C_KA + WA] = k
    o_ref[0, :, C_VA:] = p[:, C_VA:]


def _rope_tables(L):
    quarter = DQK // 4
    pos = np.arange(L)
    row = pos // GRID_W
    col = pos % GRID_W
    inv = ROPE_BASE ** (-np.arange(quarter, dtype=np.float32) / quarter)
    lane = np.arange(WA)
    c = lane % DQK
    use_col = (c // (DQK // 2)) == 1
    w = c % (DQK // 2)
    f = w % quarter
    first = w < quarter
    p = jnp.where(use_col[None, :], col[:, None], row[:, None]).astype(F32)
    ang = p * jnp.asarray(inv)[f][None, :]
    cos = jnp.cos(ang)
    sin = jnp.sin(ang)
    sa = jnp.where(first[None, :], -sin, 0.0)
    sb = jnp.where(first[None, :], 0.0, sin)
    return cos, sa, sb


def _inproj(x, mod, w_in_p, rope):
    B, L, _ = x.shape
    tl = min(L, 512)
    cos, sa, sb = _rope_tables(L)
    bm = mod.shape[0]
    mod_idx = (lambda b, i: (b, 0, 0)) if bm > 1 else (lambda b, i: (0, 0, 0))
    tab = pl.BlockSpec((tl, WA), lambda b, i: (i, 0))
    return pl.pallas_call(
        functools.partial(_inproj_kernel, rope=rope),
        grid=(B, L // tl),
        in_specs=[pl.BlockSpec((1, tl, D_MODEL), lambda b, i: (b, i, 0)),
                  pl.BlockSpec((1, 6, D_MODEL), mod_idx),
                  pl.BlockSpec((D_MODEL, N_PROJ), lambda b, i: (0, 0)),
                  tab, tab, tab],
        out_specs=pl.BlockSpec((1, tl, N_PROJ), lambda b, i: (b, i, 0)),
        out_shape=jax.ShapeDtypeStruct((B, L, N_PROJ), F32),
        compiler_params=_cparams("arbitrary", "arbitrary"),
        name="inproj",
    )(x, mod, w_in_p, cos, sa, sb)


def _permute_w_in(w_in):
    sizes = (WA, WA, WA, WB, CONV_CH, 2 * HB, WC, WC, WC, WC, 4 * HC)
    offs = np.concatenate([[0], np.cumsum(sizes)])
    qa, ka, va, z, xbc, dt, qm, km, vm, om, gm = (w_in[:, offs[i]:offs[i + 1]] for i in range(11))
    pad = jnp.zeros((w_in.shape[0], N_PROJ - C_DTGM - 2 * HB - 4 * HC), w_in.dtype)
    return jnp.concatenate([qa, ka, va, om, z, xbc, qm, km, vm, dt, gm, pad], axis=1).astype(BF16)


def _attn_kernel(lamp_ref, q_ref, kt_ref, v_ref, g_ref, o_ref, *, lam_init):
    lp = lamp_ref[...]
    lam = (jnp.exp(jnp.sum(lp[0:1] * lp[1:2], axis=-1, keepdims=True))
           - jnp.exp(jnp.sum(lp[2:3] * lp[3:4], axis=-1, keepdims=True)) + lam_init)
    q = q_ref[0]
    tq = q.shape[0]
    v = v_ref[0]
    lane_head = lax.broadcasted_iota(jnp.int32, (1, WA), 1) // DVA
    acc = jnp.zeros((tq, WA), F32)
    for h in range(HA):
        ps = []
        for m in range(2):
            c0 = (2 * h + m) * DQK
            s = jnp.dot(q[:, c0:c0 + DQK].astype(BF16), kt_ref[0, c0:c0 + DQK, :],
                        preferred_element_type=F32)
            p = jnp.exp(s - jnp.max(s, axis=-1, keepdims=True))
            ps.append((p, 1.0 / jnp.sum(p, axis=-1, keepdims=True)))
        w = ps[0][0] * ps[0][1] - ps[1][0] * (lam * ps[1][1])
        o = jnp.dot(w.astype(BF16), v, preferred_element_type=F32)
        acc = acc + jnp.where(lane_head == h, o, 0.0)
    outs = []
    for h in range(HA):
        seg = acc[:, h * DVA:(h + 1) * DVA]
        ms = jnp.mean(seg * seg, axis=-1, keepdims=True)
        outs.append(seg * lax.rsqrt(ms + LN_EPS))
    o_ref[0] = jnp.concatenate(outs, axis=1) * g_ref[...] * (1.0 - lam_init)


def _attention(proj, kt, v, lam_params, attn_g, layer):
    B, L, _ = proj.shape
    lk = kt.shape[2]
    tq = min(L, 256)
    lam_init = 0.8 - 0.6 * math.exp(-0.3 * layer)
    return pl.pallas_call(
        functools.partial(_attn_kernel, lam_init=lam_init),
        grid=(B, L // tq),
        in_specs=[pl.BlockSpec((4, DQK), lambda b, i: (0, 0)),
                  pl.BlockSpec((1, tq, WA), lambda b, i: (b, i, C_QA // WA)),
                  pl.BlockSpec((1, WA, lk), lambda b, i: (b, 0, 0)),
                  pl.BlockSpec((1, lk, WA), lambda b, i: (b, 0, 0)),
                  pl.BlockSpec((1, WA), lambda b, i: (0, 0))],
        out_specs=pl.BlockSpec((1, tq, WA), lambda b, i: (b, i, 0)),
        out_shape=jax.ShapeDtypeStruct((B, L, WA), F32),
        compiler_params=_cparams("arbitrary", "arbitrary"),
        name="diff_attention",
    )(lam_params, proj, kt, v, jnp.tile(attn_g, HA).reshape(1, WA))


def _attn_operands(proj, ctx_k, ctx_v):
    B, L, _ = proj.shape
    k = proj[:, :, C_KA:C_KA + WA]
    v = proj[:, :, C_VA:C_VA + WA]
    if ctx_k is not None:
        k = jnp.concatenate([k, ctx_k.reshape(B, -1, WA)], axis=1)
        v = jnp.concatenate([v, ctx_v.reshape(B, -1, WA)], axis=1)
    return jnp.swapaxes(k, 1, 2).astype(BF16), v.astype(BF16)


def _tri(rev):
    i = lax.broadcasted_iota(jnp.int32, (CHUNK, CHUNK), 0)
    j = lax.broadcasted_iota(jnp.int32, (CHUNK, CHUNK), 1)
    return (j >= i) if rev else (j <= i)


def _chunk_cumsum(a, rev):
    return jnp.dot(_tri(rev).astype(F32), a, precision=HIGHEST, preferred_element_type=F32)


def _dir_chunk(c, nc, d):
    return c if d == 0 else nc - 1 - c


def _ssd_dir(x_ref, xp_ref, xn_ref, dt_ref, cw_ref, cb_ref, dtb_ref, alog_ref, ht, chunk, nc, d):
    rev = d == 1
    x = x_ref[0]
    rowid = lax.broadcasted_iota(jnp.int32, (CHUNK, 1), 0)
    prev = jnp.where(chunk == 0, 0.0, xp_ref[0, SUBLANES - 1:SUBLANES, :])
    nxt = jnp.where(chunk == nc - 1, 0.0, xn_ref[0, 0:1, :])
    xm1 = jnp.where(rowid == 0, prev, pltpu.roll(x, 1, 0))
    xp1 = jnp.where(rowid == CHUNK - 1, nxt, pltpu.roll(x, CHUNK - 1, 0))
    xc = cw_ref[0:1, :] * xm1 + cw_ref[1:2, :] * x + cw_ref[2:3, :] * xp1 + cb_ref[...]
    xc = xc * _sigmoid(xc)
    xs = xc[:, :WB]

    lane = lax.broadcasted_iota(jnp.int32, (1, LANES), 1)
    dt = _softplus(dt_ref[0] + dtb_ref[...])
    aneg = jnp.where(lane < DT_LANES, -jnp.exp(alog_ref[...]), 0.0)
    cum = _chunk_cumsum(dt * aneg, rev)
    cum_t = cum.T
    tot = cum[0:1, :] if rev else cum[CHUNK - 1:CHUNK, :]
    er = lax.broadcasted_iota(jnp.int32, (LANES, WB), 0)
    ec = lax.broadcasted_iota(jnp.int32, (LANES, WB), 1)
    expand = (er == d * HB + ec // PB).astype(F32)

    def widen(t):
        return jnp.dot(t, expand, precision=HIGHEST, preferred_element_type=F32)

    dt_w = widen(dt)
    e_w = widen(jnp.exp(cum))
    te_w = widen(jnp.exp(tot - cum))
    xin = xs * dt_w
    xw = (xin * te_w).astype(BF16)
    xin_b = xin.astype(BF16)
    mask = _tri(rev)
    ht_b = ht.astype(BF16)
    ys, sts = [], []
    gw = WB // GB
    for g in range(GB):
        bm = xc[:, WB + g * NB:WB + (g + 1) * NB].astype(BF16)
        cm = xc[:, WB + GB * NB + g * NB:WB + GB * NB + (g + 1) * NB].astype(BF16)
        cb = lax.dot_general(cm, bm, (((1,), (1,)), ((), ())), preferred_element_type=F32)
        yd = []
        for r in range(HB // GB):
            h = g * (HB // GB) + r
            hd = d * HB + h
            seg = cum[:, hd:hd + 1] - cum_t[hd:hd + 1, :]
            dec = jnp.exp(jnp.where(mask, seg, -jnp.inf))
            yd.append(jnp.dot((cb * dec).astype(BF16), xin_b[:, h * PB:(h + 1) * PB],
                              preferred_element_type=F32))
        y_off = jnp.dot(cm, ht_b[:, g * gw:(g + 1) * gw], preferred_element_type=F32)
        ys.append(jnp.concatenate(yd, axis=1) + y_off * e_w[:, g * gw:(g + 1) * gw])
        sts.append(lax.dot_general(bm, xw[:, g * gw:(g + 1) * gw], (((0,), (0,)), ((), ())),
                                   preferred_element_type=F32))
    y = jnp.concatenate(ys, axis=1)
    cd = e_w[0:1, :] if rev else e_w[CHUNK - 1:CHUNK, :]
    ht_new = ht * cd + jnp.concatenate(sts, axis=1)
    return y, xs, ht_new


def _ssd_kernel(xf_ref, xfp_ref, xfn_ref, dtf_ref, xb_ref, xbp_ref, xbn_ref, dtb_ref,
                cw_ref, cb_ref, dtbias_ref, alog_ref, dskip_ref, h0_ref,
                yf_ref, yb_ref, hfin_ref, h_scr, *, nc):
    c = pl.program_id(1)

    @pl.when(c == 0)
    def _():
        h_scr[...] = h0_ref[0]

    yf, xs_f, hf = _ssd_dir(xf_ref, xfp_ref, xfn_ref, dtf_ref, cw_ref, cb_ref, dtbias_ref, alog_ref,
                            h_scr[0], c, nc, 0)
    yb, _, hb = _ssd_dir(xb_ref, xbp_ref, xbn_ref, dtb_ref, cw_ref, cb_ref, dtbias_ref, alog_ref,
                         h_scr[1], nc - 1 - c, nc, 1)
    yf_ref[0] = yf + xs_f * dskip_ref[...]
    yb_ref[0] = yb
    h_scr[0] = hf
    h_scr[1] = hb

    @pl.when(c == nc - 1)
    def _():
        hfin_ref[0] = h_scr[...]


def _pad_row(v, width=LANES):
    v = v.reshape(1, -1).astype(F32)
    return jnp.pad(v, ((0, 0), (0, width - v.shape[1])))


def _ssd(proj, conv_w, conv_b, dt_bias, a_log, d_skip, h0):
    B, L, _ = proj.shape
    nc = L // CHUNK
    r8 = CHUNK // SUBLANES
    nb8 = L // SUBLANES
    xblk = C_XBC // CONV_CH
    dblk = C_DTGM // LANES

    def specs(d):
        ch = lambda c: _dir_chunk(c, nc, d)
        return [
            pl.BlockSpec((1, CHUNK, CONV_CH), lambda b, c: (b, ch(c), xblk)),
            pl.BlockSpec((1, SUBLANES, CONV_CH), lambda b, c: (b, jnp.maximum(ch(c) * r8 - 1, 0), xblk)),
            pl.BlockSpec((1, SUBLANES, CONV_CH), lambda b, c: (b, jnp.minimum((ch(c) + 1) * r8, nb8 - 1), xblk)),
            pl.BlockSpec((1, CHUNK, LANES), lambda b, c: (b, ch(c), dblk)),
        ]

    const = lambda shape: pl.BlockSpec(shape, lambda b, c: (0,) * len(shape))
    state = pl.BlockSpec((1, 2, NB, WB), lambda b, c: (b, 0, 0, 0))
    return pl.pallas_call(
        functools.partial(_ssd_kernel, nc=nc),
        grid=(B, nc),
        in_specs=specs(0) + specs(1) + [const((CONV_W, CONV_CH)), const((1, CONV_CH)), const((1, LANES)),
                                        const((1, LANES)), const((1, WB)), state],
        out_specs=[pl.BlockSpec((1, CHUNK, WB), lambda b, c: (b, c, 0)),
                   pl.BlockSpec((1, CHUNK, WB), lambda b, c: (b, nc - 1 - c, 0)),
                   state],
        out_shape=[jax.ShapeDtypeStruct((B, L, WB), F32), jax.ShapeDtypeStruct((B, L, WB), F32),
                   jax.ShapeDtypeStruct((B, 2, NB, WB), F32)],
        scratch_shapes=[pltpu.VMEM((2, NB, WB), F32)],
        compiler_params=_cparams("arbitrary", "arbitrary"),
        name="ssd_scan",
    )(proj, proj, proj, proj, proj, proj, proj, proj,
      conv_w, conv_b.reshape(1, CONV_CH), _pad_row(dt_bias), _pad_row(a_log),
      jnp.repeat(d_skip, PB).reshape(1, WB), h0)


def _ssd_state_in(state):
    B = state.shape[0]
    return jnp.transpose(state, (0, 1, 4, 2, 3)).reshape(B, 2, NB, WB)


def _ssd_state_out(ht):
    B = ht.shape[0]
    return jnp.transpose(ht.reshape(B, 2, NB, HB, PB), (0, 1, 3, 4, 2))


CAUG = LANES


def _mlstm_dir(q_ref, k_ref, v_ref, g_ref, gbias_ref, caug, m_row, d):
    rev = d == 1
    q = q_ref[0]
    k = k_ref[0] * (DKC ** -0.5)
    v = v_ref[0]
    pre = g_ref[0] + gbias_ref[...]
    bc = _chunk_cumsum(-_softplus(-pre), rev)
    bc_t = bc.T
    pre_t = pre.T
    mask = _tri(rev)
    lane = lax.broadcasted_iota(jnp.int32, (1, LANES), 1)
    hs, cnew = [], []
    m_new_row = m_row
    for h in range(HC):
        li = GM_LANE0 + d * 2 * HC + h
        lf = li + HC
        bcol = bc[:, lf:lf + 1]
        tot = bcol[0:1, :] if rev else bcol[CHUNK - 1:CHUNK, :]
        dm = jnp.where(mask, bcol - bc_t[lf:lf + 1, :] + pre_t[li:li + 1, :], -jnp.inf)
        gend = tot - bcol + pre[:, li:li + 1]
        mloc = jnp.max(gend, axis=0, keepdims=True)
        kh = k[:, h * DKC:(h + 1) * DKC]
        kw = (kh * jnp.exp(gend - mloc)).astype(BF16)
        vt = v[:, (h // 2) * LANES:(h // 2 + 1) * LANES]
        if h % 2 == 1:
            vt = pltpu.roll(vt, DVC, 1)
        vaug = jnp.where(lane < DVC, vt, (lane == DVC).astype(F32)).astype(BF16)
        cn_loc = lax.dot_general(kw, vaug, (((0,), (0,)), ((), ())), preferred_element_type=F32)
        m_prev = m_row[:, h:h + 1]
        m_new = jnp.maximum(tot + m_prev, mloc)
        c_prev = caug[:, h * CAUG:(h + 1) * CAUG]
        cnew.append(jnp.exp(tot + m_prev - m_new) * c_prev + jnp.exp(mloc - m_new) * cn_loc)
        m_new_row = jnp.where(lane == h, m_new, m_new_row)
        inter = bcol + m_prev
        mt = jnp.maximum(inter, jnp.max(dm, axis=-1, keepdims=True))
        w_inter = jnp.exp(inter - mt)
        qh = q[:, h * DKC:(h + 1) * DKC].astype(BF16)
        qk = lax.dot_general(qh, kh.astype(BF16), (((1,), (1,)), ((), ())),
                             preferred_element_type=F32) * jnp.exp(dm - mt)
        num_in = jnp.dot(qk.astype(BF16), v[:, h * DVC:(h + 1) * DVC].astype(BF16), preferred_element_type=F32)
        den_in = jnp.sum(qk, axis=-1, keepdims=True)
        cross = jnp.dot(qh, c_prev.astype(BF16), preferred_element_type=F32)
        num = num_in + w_inter * cross[:, :DVC]
        den = den_in + w_inter * cross[:, DVC:DVC + 1]
        hs.append(num / jnp.maximum(jnp.abs(den), jnp.exp(-mt)))
    return jnp.concatenate(hs, axis=1), jnp.concatenate(cnew, axis=1), m_new_row


def _mlstm_kernel(qf_ref, kf_ref, vf_ref, gf_ref, qb_ref, kb_ref, vb_ref, gb_ref, gbias_ref, c0_ref, m0_ref,
                  hf_ref, hb_ref, cfin_ref, mfin_ref, c_scr, m_scr, *, nc):
    c = pl.program_id(1)

    @pl.when(c == 0)
    def _():
        c_scr[...] = c0_ref[0]
        m_scr[...] = m0_ref[0]

    hf, cf, mf = _mlstm_dir(qf_ref, kf_ref, vf_ref, gf_ref, gbias_ref, c_scr[0], m_scr[0], 0)
    hb, cb, mb = _mlstm_dir(qb_ref, kb_ref, vb_ref, gb_ref, gbias_ref, c_scr[1], m_scr[1], 1)
    hf_ref[0] = hf
    hb_ref[0] = hb
    c_scr[0] = cf
    c_scr[1] = cb
    m_scr[0] = mf
    m_scr[1] = mb

    @pl.when(c == nc - 1)
    def _():
        cfin_ref[0] = c_scr[...]
        mfin_ref[0] = m_scr[...]


def _mlstm(proj, ig_b, fg_b, c0, m0):
    B, L, _ = proj.shape
    nc = L // CHUNK
    gbias = jnp.concatenate([ig_b[0], fg_b[0], ig_b[1], fg_b[1]]).astype(F32)
    gbias = jnp.pad(gbias, (GM_LANE0, LANES - GM_LANE0 - 4 * HC)).reshape(1, LANES)

    def specs(d):
        ch = lambda c: _dir_chunk(c, nc, d)
        col = lambda off: pl.BlockSpec((1, CHUNK, WC), lambda b, c: (b, ch(c), off // WC))
        return [col(C_QM), col(C_KM), col(C_VM),
                pl.BlockSpec((1, CHUNK, LANES), lambda b, c: (b, ch(c), C_DTGM // LANES))]

    cstate = pl.BlockSpec((1, 2, DKC, HC * CAUG), lambda b, c: (b, 0, 0, 0))
    mstate = pl.BlockSpec((1, 2, 1, LANES), lambda b, c: (b, 0, 0, 0))
    return pl.pallas_call(
        functools.partial(_mlstm_kernel, nc=nc),
        grid=(B, nc),
        in_specs=specs(0) + specs(1) + [pl.BlockSpec((1, LANES), lambda b, c: (0, 0)), cstate, mstate],
        out_specs=[pl.BlockSpec((1, CHUNK, WC), lambda b, c: (b, c, 0)),
                   pl.BlockSpec((1, CHUNK, WC), lambda b, c: (b, nc - 1 - c, 0)),
                   cstate, mstate],
        out_shape=[jax.ShapeDtypeStruct((B, L, WC), F32), jax.ShapeDtypeStruct((B, L, WC), F32),
                   jax.ShapeDtypeStruct((B, 2, DKC, HC * CAUG), F32),
                   jax.ShapeDtypeStruct((B, 2, 1, LANES), F32)],
        scratch_shapes=[pltpu.VMEM((2, DKC, HC * CAUG), F32), pltpu.VMEM((2, 1, LANES), F32)],
        compiler_params=_cparams("arbitrary", "arbitrary"),
        name="mlstm_scan",
    )(proj, proj, proj, proj, proj, proj, proj, proj, gbias, c0, m0)


def _mlstm_state_in(C, n, m):
    B = C.shape[0]
    blk = jnp.concatenate([C, n[..., None], jnp.zeros(C.shape[:-1] + (CAUG - DVC - 1,), F32)], axis=-1)
    c0 = jnp.transpose(blk, (0, 1, 3, 2, 4)).reshape(B, 2, DKC, HC * CAUG)
    m0 = jnp.pad(m, ((0, 0), (0, 0), (0, LANES - HC))).reshape(B, 2, 1, LANES)
    return c0, m0


def _mlstm_state_out(cfin, mfin):
    B = cfin.shape[0]
    blk = jnp.transpose(cfin.reshape(B, 2, DKC, HC, CAUG), (0, 1, 3, 2, 4))
    return blk[..., :DVC], blk[..., DVC], mfin[:, :, 0, :HC]


RT_IDX0 = N_EXPERTS
RT_GATE0 = N_EXPERTS + TOP_K


def _layer_norm_rows(x, g, b):
    mu = jnp.mean(x, axis=-1, keepdims=True)
    xc = x - mu
    var = jnp.mean(xc * xc, axis=-1, keepdims=True)
    return xc * lax.rsqrt(var + LN_EPS) * g + b


def _outproj_kernel(oa_ref, yf_ref, yb_ref, z_ref, hf_ref, hb_ref, om_ref, x_ref, mod_ref, w_ref,
                    sg_ref, mg_ref, lg_ref, lb_ref, rw_ref, rb_ref, x1_ref, h2_ref, rt_ref):
    z = z_ref[0]
    y = (yf_ref[0] + yb_ref[0]) * (z * _sigmoid(z))
    gw = WB // GB
    ob = []
    for g in range(GB):
        seg = y[:, g * gw:(g + 1) * gw]
        ob.append(seg * lax.rsqrt(jnp.mean(seg * seg, axis=-1, keepdims=True) + LN_EPS))
    ob = jnp.concatenate(ob, axis=1) * sg_ref[...]
    hsum = hf_ref[0] + hb_ref[0]
    oc = []
    for h in range(HC):
        seg = hsum[:, h * DVC:(h + 1) * DVC]
        mu = jnp.mean(seg, axis=-1, keepdims=True)
        sc = seg - mu
        oc.append(sc * lax.rsqrt(jnp.mean(sc * sc, axis=-1, keepdims=True) + LN_EPS))
    oc = jnp.concatenate(oc, axis=1) * mg_ref[...] * _sigmoid(om_ref[0])
    mix_in = jnp.concatenate([oa_ref[0], ob, oc], axis=1).astype(BF16)
    mix = jnp.dot(mix_in, w_ref[...], preferred_element_type=F32)
    x1 = _layer_norm_rows(ALPHA * x_ref[0] + mod_ref[0, 2:3, :] * mix, lg_ref[...], lb_ref[...])
    x1_ref[0] = x1
    h2 = x1 * (1.0 + mod_ref[0, 4:5, :]) + mod_ref[0, 3:4, :]
    h2_ref[0] = h2
    logits = jnp.dot(h2, rw_ref[...], precision=HIGHEST, preferred_element_type=F32) + rb_ref[...]
    lane = lax.broadcasted_iota(jnp.int32, (1, LANES), 1)
    lane_f = lane.astype(F32)
    lg = jnp.where(lane < N_EXPERTS, logits, -jnp.inf)
    sels, vals, ids = [], [], []
    for _ in range(TOP_K):
        mx = jnp.max(lg, axis=-1, keepdims=True)
        first = jnp.min(jnp.where(lg == mx, lane_f, float(LANES)), axis=-1, keepdims=True)
        sel = lane_f == first
        sels.append(sel)
        vals.append(mx)
        ids.append(first)
        lg = jnp.where(sel, -jnp.inf, lg)
    es = [jnp.exp(v - vals[0]) for v in vals]
    inv = 1.0 / (es[0] + es[1] + es[2] + es[3])
    rt = jnp.zeros(logits.shape, F32)
    for k in range(TOP_K):
        gk = es[k] * inv
        rt = jnp.where(sels[k], gk, rt)
        rt = jnp.where(lane == RT_IDX0 + k, ids[k], rt)
        rt = jnp.where(lane == RT_GATE0 + k, gk, rt)
    rt_ref[0] = rt


def _outproj(oa, yf, yb, hf, hb, proj, x, mod, w_out_b, ssd_g, mlstm_g, ln_g, ln_b, router_w, router_b):
    B, L, _ = x.shape
    tl = min(L, 256)
    bm = mod.shape[0]
    mod_idx = (lambda b, i: (b, 0, 0)) if bm > 1 else (lambda b, i: (0, 0, 0))
    row = lambda w: pl.BlockSpec((1, tl, w), lambda b, i: (b, i, 0))
    const = lambda shape: pl.BlockSpec(shape, lambda b, i: (0,) * len(shape))
    rw = jnp.pad(router_w.astype(F32), ((0, 0), (0, LANES - N_EXPERTS)))
    return pl.pallas_call(
        _outproj_kernel,
        grid=(B, L // tl),
        in_specs=[row(WA), row(WB), row(WB),
                  pl.BlockSpec((1, tl, WB), lambda b, i: (b, i, C_Z // WB)),
                  row(WC), row(WC),
                  pl.BlockSpec((1, tl, WC), lambda b, i: (b, i, C_OM // WC)),
                  row(D_MODEL),
                  pl.BlockSpec((1, 6, D_MODEL), mod_idx),
                  const((D_MIX, D_MODEL)), const((1, WB)), const((1, WC)),
                  const((1, D_MODEL)), const((1, D_MODEL)), const((D_MODEL, LANES)), const((1, LANES))],
        out_specs=[row(D_MODEL), row(D_MODEL), row(LANES)],
        out_shape=[jax.ShapeDtypeStruct((B, L, D_MODEL), F32), jax.ShapeDtypeStruct((B, L, D_MODEL), F32),
                   jax.ShapeDtypeStruct((B, L, LANES), F32)],
        compiler_params=_cparams("arbitrary", "arbitrary"),
        name="outproj_ln_route",
    )(oa, yf, yb, proj, hf, hb, proj, x, mod, w_out_b, ssd_g.reshape(1, WB), mlstm_g.reshape(1, WC),
      ln_g.reshape(1, D_MODEL), ln_b.reshape(1, D_MODEL), rw, _pad_row(router_b))


def _final_ln_kernel(x_ref, ff_ref, mod_ref, g_ref, b_ref, o_ref):
    o_ref[0] = _layer_norm_rows(ALPHA * x_ref[0] + mod_ref[0, 5:6, :] * ff_ref[0], g_ref[...], b_ref[...])


def _final_ln(x1, ff, mod, ln_g, ln_b):
    B, L, _ = x1.shape
    tl = min(L, 512)
    bm = mod.shape[0]
    mod_idx = (lambda b, i: (b, 0, 0)) if bm > 1 else (lambda b, i: (0, 0, 0))
    row = pl.BlockSpec((1, tl, D_MODEL), lambda b, i: (b, i, 0))
    const = pl.BlockSpec((1, D_MODEL), lambda b, i: (0, 0))
    return pl.pallas_call(
        _final_ln_kernel,
        grid=(B, L // tl),
        in_specs=[row, row, pl.BlockSpec((1, 6, D_MODEL), mod_idx), const, const],
        out_specs=row,
        out_shape=jax.ShapeDtypeStruct((B, L, D_MODEL), F32),
        compiler_params=_cparams("arbitrary", "arbitrary"),
        name="final_ln",
    )(x1, ff, mod, ln_g.reshape(1, D_MODEL), ln_b.reshape(1, D_MODEL))


MOE_TM = 1024
MOE_R = 128
MOE_U = SUBLANES
MOE_S = MOE_R + 1
ROW_CH = D_MODEL // LANES
MOE_LMAX = MOE_TM * TOP_K + N_EXPERTS * MOE_U


def _moe_kernel(offs_ref, idx_ref, gate_ref, x_ref, wgu_ref, bgu_ref, wdn_ref, bdn_ref, o_ref,
                acc, xt, yt):
    i = pl.program_id(0)
    e = pl.program_id(1)

    @pl.when((i == 0) & (e == 0))
    def _():
        xt[...] = jnp.zeros(xt.shape, F32)

    @pl.when(e == 0)
    def _():
        acc[...] = jnp.zeros(acc.shape, F32)

    base = offs_ref[i * (N_EXPERTS + 1) + e]
    cnt = offs_ref[i * (N_EXPERTS + 1) + e + 1] - base

    def chunk_body(cix, carry):
        r0 = base + cix * MOE_R
        ngrp = lax.shift_right_logical(jnp.minimum(MOE_R, cnt - cix * MOE_R), 3)

        def gather(gi, c2):
            for u in range(MOE_U):
                mi = gi * MOE_U + u
                t = jnp.minimum(idx_ref[0, 0, r0 + mi], MOE_TM - 1)
                src = pl.multiple_of(t * ROW_CH, ROW_CH)
                xt[pl.ds(mi, ROW_CH, stride=MOE_S), :] = x_ref[pl.ds(src, ROW_CH), :]
            return c2

        lax.fori_loop(0, ngrp, gather, 0)
        x = jnp.concatenate([xt[pl.ds(j * MOE_S, MOE_R), :] for j in range(ROW_CH)], axis=1).astype(BF16)
        gu = jnp.dot(x, wgu_ref[0], preferred_element_type=F32) + bgu_ref[0]
        g = jnp.minimum(gu[:, :D_FF], SWIGLU_LIMIT)
        u_ = jnp.clip(gu[:, D_FF:], -SWIGLU_LIMIT, SWIGLU_LIMIT)
        act = g * _sigmoid(SWIGLU_ALPHA * g) * (u_ + 1.0)
        y = jnp.dot(act.astype(BF16), wdn_ref[0], preferred_element_type=F32) + bdn_ref[0]
        for j in range(ROW_CH):
            yt[pl.ds(j * MOE_S, MOE_R), :] = y[:, j * LANES:(j + 1) * LANES]

        def scatter(gi, c2):
            upd = []
            for u in range(MOE_U):
                mi = gi * MOE_U + u
                dst = pl.multiple_of(idx_ref[0, 0, r0 + mi] * ROW_CH, ROW_CH)
                gt = gate_ref[0, 0, r0 + mi]
                upd.append((dst, acc[pl.ds(dst, ROW_CH), :] + gt * yt[pl.ds(mi, ROW_CH, stride=MOE_S), :]))
            for dst, val in upd:
                acc[pl.ds(dst, ROW_CH), :] = val
            return c2

        lax.fori_loop(0, ngrp, scatter, 0)
        return carry

    lax.fori_loop(0, lax.shift_right_logical(cnt + MOE_R - 1, 7), chunk_body, 0)

    @pl.when(e == N_EXPERTS - 1)
    def _():
        o_ref[...] = acc[0:MOE_TM * ROW_CH, :]


def _moe(h2, offs, idx, gate, w_gu_b, b_gu, w_dn_b, b_dn):
    T = h2.shape[0]
    nt = T // MOE_TM
    xv = h2.reshape(T * ROW_CH, LANES)
    lst = lambda: pl.BlockSpec((1, 1, MOE_LMAX), lambda i, e, o: (i, 0, 0), memory_space=pltpu.SMEM)
    out = pl.pallas_call(
        _moe_kernel,
        grid_spec=pltpu.PrefetchScalarGridSpec(
            num_scalar_prefetch=1,
            grid=(nt, N_EXPERTS),
            in_specs=[lst(), lst(),
                      pl.BlockSpec((MOE_TM * ROW_CH, LANES), lambda i, e, o: (i, 0)),
                      pl.BlockSpec((1, D_MODEL, 2 * D_FF), lambda i, e, o: (e, 0, 0)),
                      pl.BlockSpec((1, 1, 2 * D_FF), lambda i, e, o: (e, 0, 0)),
                      pl.BlockSpec((1, D_FF, D_MODEL), lambda i, e, o: (e, 0, 0)),
                      pl.BlockSpec((1, 1, D_MODEL), lambda i, e, o: (e, 0, 0))],
            out_specs=pl.BlockSpec((MOE_TM * ROW_CH, LANES), lambda i, e, o: (i, 0)),
            scratch_shapes=[pltpu.VMEM(((MOE_TM + 1) * ROW_CH, LANES), F32),
                            pltpu.VMEM((ROW_CH * MOE_S, LANES), F32),
                            pltpu.VMEM((ROW_CH * MOE_S, LANES), F32)]),
        out_shape=jax.ShapeDtypeStruct((T * ROW_CH, LANES), F32),
        compiler_params=_cparams("arbitrary", "arbitrary"),
        name="moe_ffn",
    )(offs, idx, gate, xv, w_gu_b, b_gu.reshape(N_EXPERTS, 1, 2 * D_FF), w_dn_b,
      b_dn.reshape(N_EXPERTS, 1, D_MODEL))
    return out.reshape(T, D_MODEL)


def _routing_lists(rt):
    T = rt.shape[0]
    nt = T // MOE_TM
    n = MOE_TM * TOP_K
    ids = rt[:, RT_IDX0:RT_IDX0 + TOP_K].astype(jnp.int32).reshape(nt, n)
    gts = rt[:, RT_GATE0:RT_GATE0 + TOP_K].reshape(nt, n)
    tok = jnp.broadcast_to((jnp.arange(n, dtype=jnp.int32) // TOP_K)[None, :], (nt, n))
    key = ids * MOE_TM + tok
    key_s, gate_s = lax.sort((key, gts), dimension=1, num_keys=1)
    tok_s = key_s % MOE_TM
    bounds = jnp.arange(N_EXPERTS + 1, dtype=jnp.int32) * MOE_TM
    ustart = jax.vmap(lambda k: jnp.searchsorted(k, bounds, side='left'))(key_s).astype(jnp.int32)
    cnt = ustart[:, 1:] - ustart[:, :-1]
    pc = (cnt + MOE_U - 1) // MOE_U * MOE_U
    offs = jnp.concatenate([jnp.zeros((nt, 1), jnp.int32), jnp.cumsum(pc, axis=1).astype(jnp.int32)], axis=1)
    j = jnp.arange(MOE_LMAX, dtype=jnp.int32)
    ej = jax.vmap(lambda o: jnp.searchsorted(o[1:], j, side='right'))(offs).astype(jnp.int32)
    ej = jnp.minimum(ej, N_EXPERTS - 1)
    r = j[None, :] - jnp.take_along_axis(offs, ej, axis=1)
    valid = r < jnp.take_along_axis(cnt, ej, axis=1)
    src = jnp.clip(jnp.take_along_axis(ustart, ej, axis=1) + r, 0, n - 1)
    idx = jnp.where(valid, jnp.take_along_axis(tok_s, src, axis=1), MOE_TM)
    gate = jnp.where(valid, jnp.take_along_axis(gate_s, src, axis=1), 0.0)
    return offs.reshape(-1), idx.reshape(nt, 1, MOE_LMAX), gate.reshape(nt, 1, MOE_LMAX)


def _trunk_layer(x, mod, p, layer, ctx):
    B, L, _ = x.shape
    proj = _inproj(x, mod, p['w_in'], rope=ctx is not None)
    kt, v = _attn_operands(proj, None if ctx is None else ctx['k'], None if ctx is None else ctx['v'])
    oa = _attention(proj, kt, v, p['lam'], p['attn_g'], layer)
    if ctx is None:
        h0 = jnp.zeros((B, 2, NB, WB), F32)
        c0 = jnp.zeros((B, 2, DKC, HC * CAUG), F32)
        m0 = jnp.zeros((B, 2, 1, LANES), F32)
    else:
        h0 = _ssd_state_in(ctx['ssd'])
        c0, m0 = _mlstm_state_in(ctx['C'], ctx['n'], ctx['m'])
    yf, yb, hfin = _ssd(proj, p['conv_w'], p['conv_b'], p['dt_bias'], p['a_log'], p['d_skip'], h0)
    hf, hb, cfin, mfin = _mlstm(proj, p['ig_b'], p['fg_b'], c0, m0)
    x1, h2, rt = _outproj(oa, yf, yb, hf, hb, proj, x, mod, p['w_out'], p['ssd_g'], p['mlstm_g'],
                          p['ln1_g'], p['ln1_b'], p['router_w'], p['router_b'])
    offs, idx, gate = _routing_lists(rt.reshape(B * L, LANES))
    ff = _moe(h2.reshape(B * L, D_MODEL), offs, idx, gate, p['w_gu'], p['b_gu'], p['w_dn'], p['b_dn'])
    x2 = _final_ln(x1, ff.reshape(B, L, D_MODEL), mod, p['ln2_g'], p['ln2_b'])
    new_ctx = None
    if ctx is None:
        c_out, n_out, m_out = _mlstm_state_out(cfin, mfin)
        new_ctx = (proj[:, :, C_KA:C_KA + WA].reshape(B, L, HA, 2 * DQK),
                   proj[:, :, C_VA:C_VA + WA].reshape(B, L, HA, DVA),
                   _ssd_state_out(hfin), c_out, n_out, m_out)
    return x2, new_ctx


def kernel(x_prompt, x_sample, c, cache_attn_k, cache_attn_v, state_ssd, state_mlstm_C, state_mlstm_n, state_mlstm_m, c_ctx, w_mod, b_mod, w_in, lam_q1, lam_k1, lam_q2, lam_k2, attn_g, conv_w, conv_b, dt_bias, a_log, d_skip, ssd_g, ig_b, fg_b, mlstm_g, w_out, ln1_g, ln1_b, router_w, router_b, w_gu, b_gu, w_dn, b_dn, ln2_g, ln2_b):
    nb = c.shape[0]
    cond = jnp.concatenate([c_ctx[None, :], c, jnp.zeros((2 * SUBLANES - 1 - nb, D_MODEL), F32)], axis=0)
    y_prompt, y_sample = x_prompt, x_sample
    outs = [[] for _ in range(6)]
    for l in range(DEPTH):
        p = {'w_in': _permute_w_in(w_in[l]),
             'lam': jnp.stack([lam_q1[l], lam_k1[l], lam_q2[l], lam_k2[l]]),
             'attn_g': attn_g[l], 'conv_w': conv_w[l], 'conv_b': conv_b[l], 'dt_bias': dt_bias[l],
             'a_log': a_log[l], 'd_skip': d_skip[l], 'ssd_g': ssd_g[l], 'ig_b': ig_b[l], 'fg_b': fg_b[l],
             'mlstm_g': mlstm_g[l], 'w_out': w_out[l].astype(BF16), 'ln1_g': ln1_g[l], 'ln1_b': ln1_b[l],
             'router_w': router_w[l], 'router_b': router_b[l], 'w_gu': w_gu[l].astype(BF16), 'b_gu': b_gu[l],
             'w_dn': w_dn[l].astype(BF16), 'b_dn': b_dn[l], 'ln2_g': ln2_g[l], 'ln2_b': ln2_b[l]}
        mod = _modulation(cond, w_mod[l], b_mod[l]).reshape(2 * SUBLANES, 6, D_MODEL)
        y_prompt, st = _trunk_layer(y_prompt, mod[0:1], p, l, None)
        for acc_list, s in zip(outs, st):
            acc_list.append(s)
        ctx = {'k': cache_attn_k[:, l], 'v': cache_attn_v[:, l], 'ssd': state_ssd[:, l],
               'C': state_mlstm_C[:, l], 'n': state_mlstm_n[:, l], 'm': state_mlstm_m[:, l]}
        y_sample, _ = _trunk_layer(y_sample, mod[1:1 + nb], p, l, ctx)
    return (y_prompt, y_sample) + tuple(jnp.stack(o, axis=1) for o in outs)
```

```python
import functools
import math

import jax
import jax.numpy as jnp
import numpy as np
from jax import lax
from jax.experimental import pallas as pl
from jax.experimental.pallas import tpu as pltpu

F32 = jnp.float32
BF16 = jnp.bfloat16
HIGHEST = lax.Precision.HIGHEST

D_MODEL = 1024
DEPTH = 2
GRID_W = 64
HA = 4
DQK = 32
DVA = 2 * DQK
WA = HA * DVA
ROPE_BASE = 10000.0
HB = 8
PB = 64
WB = HB * PB
GB = 2
NB = 64
CONV_W = 3
CONV_CH = WB + 2 * GB * NB
HC = 4
DKC = 64
DVC = 64
WC = HC * DVC
D_MIX = WA + WB + WC
CHUNK = 128
N_EXPERTS = 32
TOP_K = 4
D_FF = D_MODEL
SWIGLU_LIMIT = 7.0
SWIGLU_ALPHA = 1.702
ALPHA = (2 * DEPTH) ** 0.25
LN_EPS = 1e-5

C_QA, C_KA, C_VA, C_OM = 0, 256, 512, 768
C_Z = 1024
C_XBC = 1536
C_QM, C_KM, C_VM = 2304, 2560, 2816
C_DTGM = 3072
N_PROJ = 3200
DT_LANES = 2 * HB
GM_LANE0 = DT_LANES

VMEM_LIMIT = 56 * 1024 * 1024
LANES = 128
SUBLANES = 8


def _cparams(*sem):
    return pltpu.CompilerParams(dimension_semantics=sem, vmem_limit_bytes=VMEM_LIMIT)


def _sigmoid(x):
    return 1.0 / (1.0 + jnp.exp(-x))


def _softplus(x):
    return jnp.maximum(x, 0.0) + jnp.log1p(jnp.exp(-jnp.abs(x)))


def _mod_kernel(c_ref, w_ref, b_ref, o_ref):
    c = c_ref[...]
    s = (c * _sigmoid(c)).astype(BF16)
    o_ref[...] = jnp.dot(s, w_ref[...].astype(BF16), preferred_element_type=F32) + b_ref[...]


def _modulation(cond, w_mod, b_mod):
    rows = cond.shape[0]
    n = w_mod.shape[1]
    tn = D_MODEL
    return pl.pallas_call(
        _mod_kernel,
        grid=(n // tn,),
        in_specs=[pl.BlockSpec((rows, D_MODEL), lambda j: (0, 0)),
                  pl.BlockSpec((D_MODEL, tn), lambda j: (0, j)),
                  pl.BlockSpec((1, tn), lambda j: (0, j))],
        out_specs=pl.BlockSpec((rows, tn), lambda j: (0, j)),
        out_shape=jax.ShapeDtypeStruct((rows, n), F32),
        compiler_params=_cparams("arbitrary"),
        name="modulation",
    )(cond, w_mod, b_mod.reshape(1, n))


def _inproj_kernel(x_ref, mod_ref, w_ref, cos_ref, sa_ref, sb_ref, o_ref, *, rope):
    x = x_ref[0]
    sh = mod_ref[0, 0:1, :]
    sc = mod_ref[0, 1:2, :]
    h = (x * (1.0 + sc) + sh).astype(BF16)
    p = jnp.dot(h, w_ref[...], preferred_element_type=F32)

    def rot(t):
        return (t * cos_ref[...] + pltpu.roll(t, WA - DQK // 4, 1) * sa_ref[...]
                + pltpu.roll(t, DQK // 4, 1) * sb_ref[...])

    q = p[:, C_QA:C_QA + WA]
    k = p[:, C_KA:C_KA + WA]
    if rope:
        q = rot(q)
        k = rot(k)
    o_ref[0, :, C_QA:C_QA + WA] = q * (DQK ** -0.5)
    o_ref[0, :, C_KA:C_KA + WA] = k
    o_ref[0, :, C_VA:] = p[:, C_VA:]


def _rope_tables(L):
    quarter = DQK // 4
    pos = np.arange(L)
    row = pos // GRID_W
    col = pos % GRID_W
    inv = ROPE_BASE ** (-np.arange(quarter, dtype=np.float32) / quarter)
    lane = np.arange(WA)
    c = lane % DQK
    use_col = (c // (DQK // 2)) == 1
    w = c % (DQK // 2)
    f = w % quarter
    first = w < quarter
    p = jnp.where(use_col[None, :], col[:, None], row[:, None]).astype(F32)
    ang = p * jnp.asarray(inv)[f][None, :]
    cos = jnp.cos(ang)
    sin = jnp.sin(ang)
    sa = jnp.where(first[None, :], -sin, 0.0)
    sb = jnp.where(first[None, :], 0.0, sin)
    return cos, sa, sb


def _inproj(x, mod, w_in_p, rope):
    B, L, _ = x.shape
    tl = min(L, 512)
    cos, sa, sb = _rope_tables(L)
    bm = mod.shape[0]
    mod_idx = (lambda b, i: (b, 0, 0)) if bm > 1 else (lambda b, i: (0, 0, 0))
    tab = pl.BlockSpec((tl, WA), lambda b, i: (i, 0))
    return pl.pallas_call(
        functools.partial(_inproj_kernel, rope=rope),
        grid=(B, L // tl),
        in_specs=[pl.BlockSpec((1, tl, D_MODEL), lambda b, i: (b, i, 0)),
                  pl.BlockSpec((1, 6, D_MODEL), mod_idx),
                  pl.BlockSpec((D_MODEL, N_PROJ), lambda b, i: (0, 0)),
                  tab, tab, tab],
        out_specs=pl.BlockSpec((1, tl, N_PROJ), lambda b, i: (b, i, 0)),
        out_shape=jax.ShapeDtypeStruct((B, L, N_PROJ), F32),
        compiler_params=_cparams("arbitrary", "arbitrary"),
        name="inproj",
    )(x, mod, w_in_p, cos, sa, sb)


def _permute_w_in(w_in):
    sizes = (WA, WA, WA, WB, CONV_CH, 2 * HB, WC, WC, WC, WC, 4 * HC)
    offs = np.concatenate([[0], np.cumsum(sizes)])
    qa, ka, va, z, xbc, dt, qm, km, vm, om, gm = (w_in[:, offs[i]:offs[i + 1]] for i in range(11))
    pad = jnp.zeros((w_in.shape[0], N_PROJ - C_DTGM - 2 * HB - 4 * HC), w_in.dtype)
    return jnp.concatenate([qa, ka, va, om, z, xbc, qm, km, vm, dt, gm, pad], axis=1).astype(BF16)


def _attn_kernel(lamp_ref, q_ref, kt_ref, v_ref, g_ref, o_ref, *, lam_init):
    lp = lamp_ref[...]
    lam = (jnp.exp(jnp.sum(lp[0:1] * lp[1:2], axis=-1, keepdims=True))
           - jnp.exp(jnp.sum(lp[2:3] * lp[3:4], axis=-1, keepdims=True)) + lam_init)
    q = q_ref[0]
    tq = q.shape[0]
    v = v_ref[0]
    lane_head = lax.broadcasted_iota(jnp.int32, (1, WA), 1) // DVA
    acc = jnp.zeros((tq, WA), F32)
    for h in range(HA):
        ps = []
        for m in range(2):
            c0 = (2 * h + m) * DQK
            s = jnp.dot(q[:, c0:c0 + DQK].astype(BF16), kt_ref[0, c0:c0 + DQK, :],
                        preferred_element_type=F32)
            p = jnp.exp(s - jnp.max(s, axis=-1, keepdims=True))
            ps.append((p, 1.0 / jnp.sum(p, axis=-1, keepdims=True)))
        w = ps[0][0] * ps[0][1] - ps[1][0] * (lam * ps[1][1])
        o = jnp.dot(w.astype(BF16), v, preferred_element_type=F32)
        acc = acc + jnp.where(lane_head == h, o, 0.0)
    outs = []
    for h in range(HA):
        seg = acc[:, h * DVA:(h + 1) * DVA]
        ms = jnp.mean(seg * seg, axis=-1, keepdims=True)
        outs.append(seg * lax.rsqrt(ms + LN_EPS))
    o_ref[0] = jnp.concatenate(outs, axis=1) * g_ref[...] * (1.0 - lam_init)


def _attention(proj, kt, v, lam_params, attn_g, layer):
    B, L, _ = proj.shape
    lk = kt.shape[2]
    tq = min(L, 256)
    lam_init = 0.8 - 0.6 * math.exp(-0.3 * layer)
    return pl.pallas_call(
        functools.partial(_attn_kernel, lam_init=lam_init),
        grid=(B, L // tq),
        in_specs=[pl.BlockSpec((4, DQK), lambda b, i: (0, 0)),
                  pl.BlockSpec((1, tq, WA), lambda b, i: (b, i, C_QA // WA)),
                  pl.BlockSpec((1, WA, lk), lambda b, i: (b, 0, 0)),
                  pl.BlockSpec((1, lk, WA), lambda b, i: (b, 0, 0)),
                  pl.BlockSpec((1, WA), lambda b, i: (0, 0))],
        out_specs=pl.BlockSpec((1, tq, WA), lambda b, i: (b, i, 0)),
        out_shape=jax.ShapeDtypeStruct((B, L, WA), F32),
        compiler_params=_cparams("arbitrary", "arbitrary"),
        name="diff_attention",
    )(lam_params, proj, kt, v, jnp.tile(attn_g, HA).reshape(1, WA))


def _attn_operands(proj, ctx_k, ctx_v):
    B, L, _ = proj.shape
    k = proj[:, :, C_KA:C_KA + WA]
    v = proj[:, :, C_VA:C_VA + WA]
    if ctx_k is not None:
        k = jnp.concatenate([k, ctx_k.reshape(B, -1, WA)], axis=1)
        v = jnp.concatenate([v, ctx_v.reshape(B, -1, WA)], axis=1)
    return jnp.swapaxes(k, 1, 2).astype(BF16), v.astype(BF16)


def _tri(rev):
    i = lax.broadcasted_iota(jnp.int32, (CHUNK, CHUNK), 0)
    j = lax.broadcasted_iota(jnp.int32, (CHUNK, CHUNK), 1)
    return (j >= i) if rev else (j <= i)


def _chunk_cumsum(a, rev):
    return jnp.dot(_tri(rev).astype(F32), a, precision=HIGHEST, preferred_element_type=F32)


def _dir_chunk(c, nc, d):
    return c if d == 0 else nc - 1 - c


def _ssd_dir(x_ref, xp_ref, xn_ref, dt_ref, cw_ref, cb_ref, dtb_ref, alog_ref, ht, chunk, nc, d):
    rev = d == 1
    x = x_ref[0]
    rowid = lax.broadcasted_iota(jnp.int32, (CHUNK, 1), 0)
    prev = jnp.where(chunk == 0, 0.0, xp_ref[0, SUBLANES - 1:SUBLANES, :])
    nxt = jnp.where(chunk == nc - 1, 0.0, xn_ref[0, 0:1, :])
    xm1 = jnp.where(rowid == 0, prev, pltpu.roll(x, 1, 0))
    xp1 = jnp.where(rowid == CHUNK - 1, nxt, pltpu.roll(x, CHUNK - 1, 0))
    xc = cw_ref[0:1, :] * xm1 + cw_ref[1:2, :] * x + cw_ref[2:3, :] * xp1 + cb_ref[...]
    xc = xc * _sigmoid(xc)
    xs = xc[:, :WB]

    lane = lax.broadcasted_iota(jnp.int32, (1, LANES), 1)
    dt = _softplus(dt_ref[0] + dtb_ref[...])
    aneg = jnp.where(lane < DT_LANES, -jnp.exp(alog_ref[...]), 0.0)
    cum = _chunk_cumsum(dt * aneg, rev)
    cum_t = cum.T
    tot = cum[0:1, :] if rev else cum[CHUNK - 1:CHUNK, :]
    er = lax.broadcasted_iota(jnp.int32, (LANES, WB), 0)
    ec = lax.broadcasted_iota(jnp.int32, (LANES, WB), 1)
    expand = (er == d * HB + ec // PB).astype(F32)

    def widen(t):
        return jnp.dot(t, expand, precision=HIGHEST, preferred_element_type=F32)

    dt_w = widen(dt)
    e_w = widen(jnp.exp(cum))
    te_w = widen(jnp.exp(tot - cum))
    xin = xs * dt_w
    xw = (xin * te_w).astype(BF16)
    xin_b = xin.astype(BF16)
    mask = _tri(rev)
    ht_b = ht.astype(BF16)
    ys, sts = [], []
    gw = WB // GB
    for g in range(GB):
        bm = xc[:, WB + g * NB:WB + (g + 1) * NB].astype(BF16)
        cm = xc[:, WB + GB * NB + g * NB:WB + GB * NB + (g + 1) * NB].astype(BF16)
        cb = lax.dot_general(cm, bm, (((1,), (1,)), ((), ())), preferred_element_type=F32)
        yd = []
        for r in range(HB // GB):
            h = g * (HB // GB) + r
            hd = d * HB + h
            seg = cum[:, hd:hd + 1] - cum_t[hd:hd + 1, :]
            dec = jnp.exp(jnp.where(mask, seg, -jnp.inf))
            yd.append(jnp.dot((cb * dec).astype(BF16), xin_b[:, h * PB:(h + 1) * PB],
                              preferred_element_type=F32))
        y_off = jnp.dot(cm, ht_b[:, g * gw:(g + 1) * gw], preferred_element_type=F32)
        ys.append(jnp.concatenate(yd, axis=1) + y_off * e_w[:, g * gw:(g + 1) * gw])
        sts.append(lax.dot_general(bm, xw[:, g * gw:(g + 1) * gw], (((0,), (0,)), ((), ())),
                                   preferred_element_type=F32))
    y = jnp.concatenate(ys, axis=1)
    cd = e_w[0:1, :] if rev else e_w[CHUNK - 1:CHUNK, :]
    ht_new = ht * cd + jnp.concatenate(sts, axis=1)
    return y, xs, ht_new


def _ssd_kernel(xf_ref, xfp_ref, xfn_ref, dtf_ref, xb_ref, xbp_ref, xbn_ref, dtb_ref,
                cw_ref, cb_ref, dtbias_ref, alog_ref, dskip_ref, h0_ref,
                yf_ref, yb_ref, hfin_ref, h_scr, *, nc):
    c = pl.program_id(1)

    @pl.when(c == 0)
    def _():
        h_scr[...] = h0_ref[0]

    yf, xs_f, hf = _ssd_dir(xf_ref, xfp_ref, xfn_ref, dtf_ref, cw_ref, cb_ref, dtbias_ref, alog_ref,
                            h_scr[0], c, nc, 0)
    yb, _, hb = _ssd_dir(xb_ref, xbp_ref, xbn_ref, dtb_ref, cw_ref, cb_ref, dtbias_ref, alog_ref,
                         h_scr[1], nc - 1 - c, nc, 1)
    yf_ref[0] = yf + xs_f * dskip_ref[...]
    yb_ref[0] = yb
    h_scr[0] = hf
    h_scr[1] = hb

    @pl.when(c == nc - 1)
    def _():
        hfin_ref[0] = h_scr[...]


def _pad_row(v, width=LANES):
    v = v.reshape(1, -1).astype(F32)
    return jnp.pad(v, ((0, 0), (0, width - v.shape[1])))


def _ssd(proj, conv_w, conv_b, dt_bias, a_log, d_skip, h0):
    B, L, _ = proj.shape
    nc = L // CHUNK
    r8 = CHUNK // SUBLANES
    nb8 = L // SUBLANES
    xblk = C_XBC // CONV_CH
    dblk = C_DTGM // LANES

    def specs(d):
        ch = lambda c: _dir_chunk(c, nc, d)
        return [
            pl.BlockSpec((1, CHUNK, CONV_CH), lambda b, c: (b, ch(c), xblk)),
            pl.BlockSpec((1, SUBLANES, CONV_CH), lambda b, c: (b, jnp.maximum(ch(c) * r8 - 1, 0), xblk)),
            pl.BlockSpec((1, SUBLANES, CONV_CH), lambda b, c: (b, jnp.minimum((ch(c) + 1) * r8, nb8 - 1), xblk)),
            pl.BlockSpec((1, CHUNK, LANES), lambda b, c: (b, ch(c), dblk)),
        ]

    const = lambda shape: pl.BlockSpec(shape, lambda b, c: (0,) * len(shape))
    state = pl.BlockSpec((1, 2, NB, WB), lambda b, c: (b, 0, 0, 0))
    return pl.pallas_call(
        functools.partial(_ssd_kernel, nc=nc),
        grid=(B, nc),
        in_specs=specs(0) + specs(1) + [const((CONV_W, CONV_CH)), const((1, CONV_CH)), const((1, LANES)),
                                        const((1, LANES)), const((1, WB)), state],
        out_specs=[pl.BlockSpec((1, CHUNK, WB), lambda b, c: (b, c, 0)),
                   pl.BlockSpec((1, CHUNK, WB), lambda b, c: (b, nc - 1 - c, 0)),
                   state],
        out_shape=[jax.ShapeDtypeStruct((B, L, WB), F32), jax.ShapeDtypeStruct((B, L, WB), F32),
                   jax.ShapeDtypeStruct((B, 2, NB, WB), F32)],
        scratch_shapes=[pltpu.VMEM((2, NB, WB), F32)],
        compiler_params=_cparams("arbitrary", "arbitrary"),
        name="ssd_scan",
    )(proj, proj, proj, proj, proj, proj, proj, proj,
      conv_w, conv_b.reshape(1, CONV_CH), _pad_row(dt_bias), _pad_row(a_log),
      jnp.repeat(d_skip, PB).reshape(1, WB), h0)


def _ssd_state_in(state):
    B = state.shape[0]
    return jnp.transpose(state, (0, 1, 4, 2, 3)).reshape(B, 2, NB, WB)


def _ssd_state_out(ht):
    B = ht.shape[0]
    return jnp.transpose(ht.reshape(B, 2, NB, HB, PB), (0, 1, 3, 4, 2))


CAUG = LANES


def _mlstm_dir(q_ref, k_ref, v_ref, g_ref, gbias_ref, caug, m_row, d):
    rev = d == 1
    q = q_ref[0]
    k = k_ref[0] * (DKC ** -0.5)
    v = v_ref[0]
    pre = g_ref[0] + gbias_ref[...]
    bc = _chunk_cumsum(-_softplus(-pre), rev)
    bc_t = bc.T
    pre_t = pre.T
    mask = _tri(rev)
    lane = lax.broadcasted_iota(jnp.int32, (1, LANES), 1)
    hs, cnew = [], []
    m_new_row = m_row
    for h in range(HC):
        li = GM_LANE0 + d * 2 * HC + h
        lf = li + HC
        bcol = bc[:, lf:lf + 1]
        tot = bcol[0:1, :] if rev else bcol[CHUNK - 1:CHUNK, :]
        dm = jnp.where(mask, bcol - bc_t[lf:lf + 1, :] + pre_t[li:li + 1, :], -jnp.inf)
        gend = tot - bcol + pre[:, li:li + 1]
        mloc = jnp.max(gend, axis=0, keepdims=True)
        kh = k[:, h * DKC:(h + 1) * DKC]
        kw = (kh * jnp.exp(gend - mloc)).astype(BF16)
        vt = v[:, (h // 2) * LANES:(h // 2 + 1) * LANES]
        if h % 2 == 1:
            vt = pltpu.roll(vt, DVC, 1)
        vaug = jnp.where(lane < DVC, vt, (lane == DVC).astype(F32)).astype(BF16)
        cn_loc = lax.dot_general(kw, vaug, (((0,), (0,)), ((), ())), preferred_element_type=F32)
        m_prev = m_row[:, h:h + 1]
        m_new = jnp.maximum(tot + m_prev, mloc)
        c_prev = caug[:, h * CAUG:(h + 1) * CAUG]
        cnew.append(jnp.exp(tot + m_prev - m_new) * c_prev + jnp.exp(mloc - m_new) * cn_loc)
        m_new_row = jnp.where(lane == h, m_new, m_new_row)
        inter = bcol + m_prev
        mt = jnp.maximum(inter, jnp.max(dm, axis=-1, keepdims=True))
        w_inter = jnp.exp(inter - mt)
        qh = q[:, h * DKC:(h + 1) * DKC].astype(BF16)
        qk = lax.dot_general(qh, kh.astype(BF16), (((1,), (1,)), ((), ())),
                             preferred_element_type=F32) * jnp.exp(dm - mt)
        num_in = jnp.dot(qk.astype(BF16), v[:, h * DVC:(h + 1) * DVC].astype(BF16), preferred_element_type=F32)
        den_in = jnp.sum(qk, axis=-1, keepdims=True)
        cross = jnp.dot(qh, c_prev.astype(BF16), preferred_element_type=F32)
        num = num_in + w_inter * cross[:, :DVC]
        den = den_in + w_inter * cross[:, DVC:DVC + 1]
        hs.append(num / jnp.maximum(jnp.abs(den), jnp.exp(-mt)))
    return jnp.concatenate(hs, axis=1), jnp.concatenate(cnew, axis=1), m_new_row


def _mlstm_kernel(qf_ref, kf_ref, vf_ref, gf_ref, qb_ref, kb_ref, vb_ref, gb_ref, gbias_ref, c0_ref, m0_ref,
                  hf_ref, hb_ref, cfin_ref, mfin_ref, c_scr, m_scr, *, nc):
    c = pl.program_id(1)

    @pl.when(c == 0)
    def _():
        c_scr[...] = c0_ref[0]
        m_scr[...] = m0_ref[0]

    hf, cf, mf = _mlstm_dir(qf_ref, kf_ref, vf_ref, gf_ref, gbias_ref, c_scr[0], m_scr[0], 0)
    hb, cb, mb = _mlstm_dir(qb_ref, kb_ref, vb_ref, gb_ref, gbias_ref, c_scr[1], m_scr[1], 1)
    hf_ref[0] = hf
    hb_ref[0] = hb
    c_scr[0] = cf
    c_scr[1] = cb
    m_scr[0] = mf
    m_scr[1] = mb

    @pl.when(c == nc - 1)
    def _():
        cfin_ref[0] = c_scr[...]
        mfin_ref[0] = m_scr[...]


def _mlstm(proj, ig_b, fg_b, c0, m0):
    B, L, _ = proj.shape
    nc = L // CHUNK
    gbias = jnp.concatenate([ig_b[0], fg_b[0], ig_b[1], fg_b[1]]).astype(F32)
    gbias = jnp.pad(gbias, (GM_LANE0, LANES - GM_LANE0 - 4 * HC)).reshape(1, LANES)

    def specs(d):
        ch = lambda c: _dir_chunk(c, nc, d)
        col = lambda off: pl.BlockSpec((1, CHUNK, WC), lambda b, c: (b, ch(c), off // WC))
        return [col(C_QM), col(C_KM), col(C_VM),
                pl.BlockSpec((1, CHUNK, LANES), lambda b, c: (b, ch(c), C_DTGM // LANES))]

    cstate = pl.BlockSpec((1, 2, DKC, HC * CAUG), lambda b, c: (b, 0, 0, 0))
    mstate = pl.BlockSpec((1, 2, 1, LANES), lambda b, c: (b, 0, 0, 0))
    return pl.pallas_call(
        functools.partial(_mlstm_kernel, nc=nc),
        grid=(B, nc),
        in_specs=specs(0) + specs(1) + [pl.BlockSpec((1, LANES), lambda b, c: (0, 0)), cstate, mstate],
        out_specs=[pl.BlockSpec((1, CHUNK, WC), lambda b, c: (b, c, 0)),
                   pl.BlockSpec((1, CHUNK, WC), lambda b, c: (b, nc - 1 - c, 0)),
                   cstate, mstate],
        out_shape=[jax.ShapeDtypeStruct((B, L, WC), F32), jax.ShapeDtypeStruct((B, L, WC), F32),
                   jax.ShapeDtypeStruct((B, 2, DKC, HC * CAUG), F32),
                   jax.ShapeDtypeStruct((B, 2, 1, LANES), F32)],
        scratch_shapes=[pltpu.VMEM((2, DKC, HC * CAUG), F32), pltpu.VMEM((2, 1, LANES), F32)],
        compiler_params=_cparams("arbitrary", "arbitrary"),
        name="mlstm_scan",
    )(proj, proj, proj, proj, proj, proj, proj, proj, gbias, c0, m0)


def _mlstm_state_in(C, n, m):
    B = C.shape[0]
    blk = jnp.concatenate([C, n[..., None], jnp.zeros(C.shape[:-1] + (CAUG - DVC - 1,), F32)], axis=-1)
    c0 = jnp.transpose(blk, (0, 1, 3, 2, 4)).reshape(B, 2, DKC, HC * CAUG)
    m0 = jnp.pad(m, ((0, 0), (0, 0), (0, LANES - HC))).reshape(B, 2, 1, LANES)
    return c0, m0


def _mlstm_state_out(cfin, mfin):
    B = cfin.shape[0]
    blk = jnp.transpose(cfin.reshape(B, 2, DKC, HC, CAUG), (0, 1, 3, 2, 4))
    return blk[..., :DVC], blk[..., DVC], mfin[:, :, 0, :HC]


RT_IDX0 = 0
RT_GATE0 = TOP_K
RT_ROWS = 2 * TOP_K


def _layer_norm_rows(x, g, b):
    mu = jnp.mean(x, axis=-1, keepdims=True)
    xc = x - mu
    var = jnp.mean(xc * xc, axis=-1, keepdims=True)
    return xc * lax.rsqrt(var + LN_EPS) * g + b


def _outproj_kernel(oa_ref, yf_ref, yb_ref, z_ref, hf_ref, hb_ref, om_ref, x_ref, mod_ref, w_ref,
                    sg_ref, mg_ref, lg_ref, lb_ref, rw_ref, rb_ref, x1_ref, h2_ref, rt_ref):
    z = z_ref[0]
    y = (yf_ref[0] + yb_ref[0]) * (z * _sigmoid(z))
    gw = WB // GB
    ob = []
    for g in range(GB):
        seg = y[:, g * gw:(g + 1) * gw]
        ob.append(seg * lax.rsqrt(jnp.mean(seg * seg, axis=-1, keepdims=True) + LN_EPS))
    ob = jnp.concatenate(ob, axis=1) * sg_ref[...]
    hsum = hf_ref[0] + hb_ref[0]
    oc = []
    for h in range(HC):
        seg = hsum[:, h * DVC:(h + 1) * DVC]
        mu = jnp.mean(seg, axis=-1, keepdims=True)
        sc = seg - mu
        oc.append(sc * lax.rsqrt(jnp.mean(sc * sc, axis=-1, keepdims=True) + LN_EPS))
    oc = jnp.concatenate(oc, axis=1) * mg_ref[...] * _sigmoid(om_ref[0])
    mix_in = jnp.concatenate([oa_ref[0], ob, oc], axis=1).astype(BF16)
    mix = jnp.dot(mix_in, w_ref[...], preferred_element_type=F32)
    x1 = _layer_norm_rows(ALPHA * x_ref[0] + mod_ref[0, 2:3, :] * mix, lg_ref[...], lb_ref[...])
    x1_ref[0] = x1
    h2 = x1 * (1.0 + mod_ref[0, 4:5, :]) + mod_ref[0, 3:4, :]
    h2_ref[0] = h2
    logits = jnp.dot(h2, rw_ref[...], precision=HIGHEST, preferred_element_type=F32) + rb_ref[...]
    lane = lax.broadcasted_iota(jnp.int32, (1, LANES), 1)
    lane_f = lane.astype(F32)
    lg = jnp.where(lane < N_EXPERTS, logits, -jnp.inf)
    vals, ids = [], []
    for _ in range(TOP_K):
        mx = jnp.max(lg, axis=-1, keepdims=True)
        first = jnp.min(jnp.where(lg == mx, lane_f, float(LANES)), axis=-1, keepdims=True)
        vals.append(mx)
        ids.append(first)
        lg = jnp.where(lane_f == first, -jnp.inf, lg)
    es = [jnp.exp(v - vals[0]) for v in vals]
    inv = 1.0 / (es[0] + es[1] + es[2] + es[3])
    rt = jnp.zeros(logits.shape, F32)
    for k in range(TOP_K):
        rt = jnp.where(lane == RT_IDX0 + k, ids[k], rt)
        rt = jnp.where(lane == RT_GATE0 + k, es[k] * inv, rt)
    rt_ref[...] = rt.T[0:RT_ROWS, :]


def _outproj(oa, yf, yb, hf, hb, proj, x, mod, w_out_b, ssd_g, mlstm_g, ln_g, ln_b, router_w, router_b):
    B, L, _ = x.shape
    tl = min(L, 256)
    bm = mod.shape[0]
    mod_idx = (lambda b, i: (b, 0, 0)) if bm > 1 else (lambda b, i: (0, 0, 0))
    row = lambda w: pl.BlockSpec((1, tl, w), lambda b, i: (b, i, 0))
    const = lambda shape: pl.BlockSpec(shape, lambda b, i: (0,) * len(shape))
    rw = jnp.pad(router_w.astype(F32), ((0, 0), (0, LANES - N_EXPERTS)))
    return pl.pallas_call(
        _outproj_kernel,
        grid=(B, L // tl),
        in_specs=[row(WA), row(WB), row(WB),
                  pl.BlockSpec((1, tl, WB), lambda b, i: (b, i, C_Z // WB)),
                  row(WC), row(WC),
                  pl.BlockSpec((1, tl, WC), lambda b, i: (b, i, C_OM // WC)),
                  row(D_MODEL),
                  pl.BlockSpec((1, 6, D_MODEL), mod_idx),
                  const((D_MIX, D_MODEL)), const((1, WB)), const((1, WC)),
                  const((1, D_MODEL)), const((1, D_MODEL)), const((D_MODEL, LANES)), const((1, LANES))],
        out_specs=[row(D_MODEL), row(D_MODEL),
                   pl.BlockSpec((RT_ROWS, tl), lambda b, i: (0, b * (L // tl) + i))],
        out_shape=[jax.ShapeDtypeStruct((B, L, D_MODEL), F32), jax.ShapeDtypeStruct((B, L, D_MODEL), F32),
                   jax.ShapeDtypeStruct((RT_ROWS, B * L), F32)],
        compiler_params=_cparams("arbitrary", "arbitrary"),
        name="outproj_ln_route",
    )(oa, yf, yb, proj, hf, hb, proj, x, mod, w_out_b, ssd_g.reshape(1, WB), mlstm_g.reshape(1, WC),
      ln_g.reshape(1, D_MODEL), ln_b.reshape(1, D_MODEL), rw, _pad_row(router_b))


def _final_ln_kernel(x_ref, ff_ref, mod_ref, g_ref, b_ref, o_ref):
    o_ref[0] = _layer_norm_rows(ALPHA * x_ref[0] + mod_ref[0, 5:6, :] * ff_ref[0], g_ref[...], b_ref[...])


def _final_ln(x1, ff, mod, ln_g, ln_b):
    B, L, _ = x1.shape
    tl = min(L, 512)
    bm = mod.shape[0]
    mod_idx = (lambda b, i: (b, 0, 0)) if bm > 1 else (lambda b, i: (0, 0, 0))
    row = pl.BlockSpec((1, tl, D_MODEL), lambda b, i: (b, i, 0))
    const = pl.BlockSpec((1, D_MODEL), lambda b, i: (0, 0))
    return pl.pallas_call(
        _final_ln_kernel,
        grid=(B, L // tl),
        in_specs=[row, row, pl.BlockSpec((1, 6, D_MODEL), mod_idx), const, const],
        out_specs=row,
        out_shape=jax.ShapeDtypeStruct((B, L, D_MODEL), F32),
        compiler_params=_cparams("arbitrary", "arbitrary"),
        name="final_ln",
    )(x1, ff, mod, ln_g.reshape(1, D_MODEL), ln_b.reshape(1, D_MODEL))


MOE_TM = 1024
MOE_R = 128
MOE_U = SUBLANES
MOE_S = MOE_R + 1
ROW_CH = D_MODEL // LANES
MOE_LMAX = MOE_TM * TOP_K + N_EXPERTS * MOE_U


def _ceil_to_unit(x):
    return jnp.floor((x + (MOE_U - 1.0)) * (1.0 / MOE_U)) * float(MOE_U)


def _build_routing(rt_ref, d_vm, d_sm, g_sm, idx_sm, gate_sm, sems):
    r = rt_ref[...]
    eio = lax.broadcasted_iota(jnp.int32, (LANES, MOE_TM), 0).astype(F32)
    hit = [r[k:k + 1, :] == eio for k in range(TOP_K)]
    sel = jnp.zeros((LANES, MOE_TM), F32)
    for k in range(TOP_K):
        sel = jnp.where(hit[k], 1.0, sel)
    sel_b = sel.astype(BF16)
    pc_col = _ceil_to_unit(jnp.sum(sel, axis=1, keepdims=True))
    er = lax.broadcasted_iota(jnp.int32, (LANES, LANES), 0)
    ec = lax.broadcasted_iota(jnp.int32, (LANES, LANES), 1)
    offs_col = jnp.dot((ec < er).astype(F32), jnp.broadcast_to(pc_col, (LANES, LANES)),
                       precision=HIGHEST, preferred_element_type=F32)[:, 0:1]
    cnt_row = lax.dot_general(jnp.ones((SUBLANES, MOE_TM), BF16), sel_b, (((1,), (1,)), ((), ())),
                              preferred_element_type=F32)
    offs_row = jnp.dot(_ceil_to_unit(cnt_row), (er < ec).astype(F32), precision=HIGHEST,
                       preferred_element_type=F32)
    tr = lax.broadcasted_iota(jnp.int32, (MOE_TM, MOE_TM), 0)
    tc = lax.broadcasted_iota(jnp.int32, (MOE_TM, MOE_TM), 1)
    rank = jnp.dot(sel_b, (tr < tc).astype(BF16), preferred_element_type=F32)
    dest = offs_col + rank
    dest4 = [jnp.sum(jnp.where(hit[k], dest, 0.0), axis=0, keepdims=True) for k in range(TOP_K)]
    tail = jnp.concatenate([offs_row[0:RT_ROWS - TOP_K, :],
                            jnp.zeros((RT_ROWS - TOP_K, MOE_TM - LANES), F32)], axis=1)
    d_vm[...] = jnp.concatenate(dest4 + [tail], axis=0).astype(jnp.int32)
    cp_d = pltpu.make_async_copy(d_vm, d_sm, sems.at[0])
    cp_g = pltpu.make_async_copy(rt_ref, g_sm, sems.at[1])
    cp_d.start()
    cp_g.start()
    cp_d.wait()
    cp_g.wait()

    def pad_body(ex, c):
        end = d_sm[TOP_K, ex + 1]
        for u in range(MOE_U):
            pos = jnp.maximum(end - 1 - u, 0)
            idx_sm[pos] = MOE_TM
            gate_sm[pos] = 0.0
        return c

    lax.fori_loop(0, N_EXPERTS, pad_body, 0)

    def place_body(t, c):
        for k in range(TOP_K):
            pos = d_sm[k, t]
            idx_sm[pos] = t
            gate_sm[pos] = g_sm[RT_GATE0 + k, t]
        return c

    lax.fori_loop(0, MOE_TM, place_body, 0)


def _moe_kernel(rt_ref, x_ref, wgu_ref, bgu_ref, wdn_ref, bdn_ref, o_ref,
                acc, xt, yt, d_vm, d_sm, g_sm, idx_sm, gate_sm, sems):
    i = pl.program_id(0)
    e = pl.program_id(1)

    @pl.when((i == 0) & (e == 0))
    def _():
        xt[...] = jnp.zeros(xt.shape, F32)

    @pl.when(e == 0)
    def _():
        acc[...] = jnp.zeros(acc.shape, F32)
        _build_routing(rt_ref, d_vm, d_sm, g_sm, idx_sm, gate_sm, sems)

    base = d_sm[TOP_K, e]
    cnt = d_sm[TOP_K, e + 1] - base

    def chunk_body(cix, carry):
        r0 = base + cix * MOE_R
        ngrp = lax.shift_right_logical(jnp.minimum(MOE_R, cnt - cix * MOE_R), 3)

        def gather(gi, c2):
            for u in range(MOE_U):
                mi = gi * MOE_U + u
                t = jnp.minimum(idx_sm[r0 + mi], MOE_TM - 1)
                src = pl.multiple_of(t * ROW_CH, ROW_CH)
                xt[pl.ds(mi, ROW_CH, stride=MOE_S), :] = x_ref[pl.ds(src, ROW_CH), :]
            return c2

        lax.fori_loop(0, ngrp, gather, 0)
        x = jnp.concatenate([xt[pl.ds(j * MOE_S, MOE_R), :] for j in range(ROW_CH)], axis=1).astype(BF16)
        gu = jnp.dot(x, wgu_ref[0], preferred_element_type=F32) + bgu_ref[0]
        g = jnp.minimum(gu[:, :D_FF], SWIGLU_LIMIT)
        u_ = jnp.clip(gu[:, D_FF:], -SWIGLU_LIMIT, SWIGLU_LIMIT)
        act = g * _sigmoid(SWIGLU_ALPHA * g) * (u_ + 1.0)
        y = jnp.dot(act.astype(BF16), wdn_ref[0], preferred_element_type=F32) + bdn_ref[0]
        for j in range(ROW_CH):
            yt[pl.ds(j * MOE_S, MOE_R), :] = y[:, j * LANES:(j + 1) * LANES]

        def scatter(gi, c2):
            upd = []
            for u in range(MOE_U):
                mi = gi * MOE_U + u
                dst = pl.multiple_of(idx_sm[r0 + mi] * ROW_CH, ROW_CH)
                gt = gate_sm[r0 + mi]
                upd.append((dst, acc[pl.ds(dst, ROW_CH), :] + gt * yt[pl.ds(mi, ROW_CH, stride=MOE_S), :]))
            for dst, val in upd:
                acc[pl.ds(dst, ROW_CH), :] = val
            return c2

        lax.fori_loop(0, ngrp, scatter, 0)
        return carry

    lax.fori_loop(0, lax.shift_right_logical(cnt + MOE_R - 1, 7), chunk_body, 0)

    @pl.when(e == N_EXPERTS - 1)
    def _():
        o_ref[...] = acc[0:MOE_TM * ROW_CH, :]


def _moe(h2, rt, w_gu_b, b_gu, w_dn_b, b_dn):
    T = h2.shape[0]
    nt = T // MOE_TM
    xv = h2.reshape(T * ROW_CH, LANES)
    out = pl.pallas_call(
        _moe_kernel,
        grid=(nt, N_EXPERTS),
        in_specs=[pl.BlockSpec((RT_ROWS, MOE_TM), lambda i, e: (0, i)),
                  pl.BlockSpec((MOE_TM * ROW_CH, LANES), lambda i, e: (i, 0)),
                  pl.BlockSpec((1, D_MODEL, 2 * D_FF), lambda i, e: (e, 0, 0)),
                  pl.BlockSpec((1, 1, 2 * D_FF), lambda i, e: (e, 0, 0)),
                  pl.BlockSpec((1, D_FF, D_MODEL), lambda i, e: (e, 0, 0)),
                  pl.BlockSpec((1, 1, D_MODEL), lambda i, e: (e, 0, 0))],
        out_specs=pl.BlockSpec((MOE_TM * ROW_CH, LANES), lambda i, e: (i, 0)),
        scratch_shapes=[pltpu.VMEM(((MOE_TM + 1) * ROW_CH, LANES), F32),
                        pltpu.VMEM((ROW_CH * MOE_S, LANES), F32),
                        pltpu.VMEM((ROW_CH * MOE_S, LANES), F32),
                        pltpu.VMEM((RT_ROWS, MOE_TM), jnp.int32),
                        pltpu.SMEM((RT_ROWS, MOE_TM), jnp.int32),
                        pltpu.SMEM((RT_ROWS, MOE_TM), F32),
                        pltpu.SMEM((MOE_LMAX,), jnp.int32),
                        pltpu.SMEM((MOE_LMAX,), F32),
                        pltpu.SemaphoreType.DMA((2,))],
        out_shape=jax.ShapeDtypeStruct((T * ROW_CH, LANES), F32),
        compiler_params=_cparams("arbitrary", "arbitrary"),
        name="moe_ffn",
    )(rt, xv, w_gu_b, b_gu.reshape(N_EXPERTS, 1, 2 * D_FF), w_dn_b, b_dn.reshape(N_EXPERTS, 1, D_MODEL))
    return out.reshape(T, D_MODEL)


def _trunk_layer(x, mod, p, layer, ctx):
    B, L, _ = x.shape
    proj = _inproj(x, mod, p['w_in'], rope=ctx is not None)
    kt, v = _attn_operands(proj, None if ctx is None else ctx['k'], None if ctx is None else ctx['v'])
    oa = _attention(proj, kt, v, p['lam'], p['attn_g'], layer)
    if ctx is None:
        h0 = jnp.zeros((B, 2, NB, WB), F32)
        c0 = jnp.zeros((B, 2, DKC, HC * CAUG), F32)
        m0 = jnp.zeros((B, 2, 1, LANES), F32)
    else:
        h0 = _ssd_state_in(ctx['ssd'])
        c0, m0 = _mlstm_state_in(ctx['C'], ctx['n'], ctx['m'])
    yf, yb, hfin = _ssd(proj, p['conv_w'], p['conv_b'], p['dt_bias'], p['a_log'], p['d_skip'], h0)
    hf, hb, cfin, mfin = _mlstm(proj, p['ig_b'], p['fg_b'], c0, m0)
    x1, h2, rt = _outproj(oa, yf, yb, hf, hb, proj, x, mod, p['w_out'], p['ssd_g'], p['mlstm_g'],
                          p['ln1_g'], p['ln1_b'], p['router_w'], p['router_b'])
    ff = _moe(h2.reshape(B * L, D_MODEL), rt, p['w_gu'], p['b_gu'], p['w_dn'], p['b_dn'])
    x2 = _final_ln(x1, ff.reshape(B, L, D_MODEL), mod, p['ln2_g'], p['ln2_b'])
    new_ctx = None
    if ctx is None:
        c_out, n_out, m_out = _mlstm_state_out(cfin, mfin)
        new_ctx = (proj[:, :, C_KA:C_KA + WA].reshape(B, L, HA, 2 * DQK),
                   proj[:, :, C_VA:C_VA + WA].reshape(B, L, HA, DVA),
                   _ssd_state_out(hfin), c_out, n_out, m_out)
    return x2, new_ctx


def kernel(x_prompt, x_sample, c, cache_attn_k, cache_attn_v, state_ssd, state_mlstm_C, state_mlstm_n, state_mlstm_m, c_ctx, w_mod, b_mod, w_in, lam_q1, lam_k1, lam_q2, lam_k2, attn_g, conv_w, conv_b, dt_bias, a_log, d_skip, ssd_g, ig_b, fg_b, mlstm_g, w_out, ln1_g, ln1_b, router_w, router_b, w_gu, b_gu, w_dn, b_dn, ln2_g, ln2_b):
    nb = c.shape[0]
    cond = jnp.concatenate([c_ctx[None, :], c, jnp.zeros((2 * SUBLANES - 1 - nb, D_MODEL), F32)], axis=0)
    y_prompt, y_sample = x_prompt, x_sample
    outs = [[] for _ in range(6)]
    for l in range(DEPTH):
        p = {'w_in': _permute_w_in(w_in[l]),
             'lam': jnp.stack([lam_q1[l], lam_k1[l], lam_q2[l], lam_k2[l]]),
             'attn_g': attn_g[l], 'conv_w': conv_w[l], 'conv_b': conv_b[l], 'dt_bias': dt_bias[l],
             'a_log': a_log[l], 'd_skip': d_skip[l], 'ssd_g': ssd_g[l], 'ig_b': ig_b[l], 'fg_b': fg_b[l],
             'mlstm_g': mlstm_g[l], 'w_out': w_out[l].astype(BF16), 'ln1_g': ln1_g[l], 'ln1_b': ln1_b[l],
             'router_w': router_w[l], 'router_b': router_b[l], 'w_gu': w_gu[l].astype(BF16), 'b_gu': b_gu[l],
             'w_dn': w_dn[l].astype(BF16), 'b_dn': b_dn[l], 'ln2_g': ln2_g[l], 'ln2_b': ln2_b[l]}
        mod = _modulation(cond, w_mod[l], b_mod[l]).reshape(2 * SUBLANES, 6, D_MODEL)
        y_prompt, st = _trunk_layer(y_prompt, mod[0:1], p, l, None)
        for acc_list, s in zip(outs, st):
            acc_list.append(s)
        ctx = {'k': cache_attn_k[:, l], 'v': cache_attn_v[:, l], 'ssd': state_ssd[:, l],
               'C': state_mlstm_C[:, l], 'n': state_mlstm_n[:, l], 'm': state_mlstm_m[:, l]}
        y_sample, _ = _trunk_layer(y_sample, mod[1:1 + nb], p, l, ctx)
    return (y_prompt, y_sample) + tuple(jnp.stack(o, axis=1) for o in outs)
```

```python
import functools
import math

import jax
import jax.numpy as jnp
import numpy as np
from jax import lax
from jax.experimental import pallas as pl
from jax.experimental.pallas import tpu as pltpu

F32 = jnp.float32
BF16 = jnp.bfloat16
HIGHEST = lax.Precision.HIGHEST

D_MODEL = 1024
DEPTH = 2
GRID_W = 64
HA = 4
DQK = 32
DVA = 2 * DQK
WA = HA * DVA
ROPE_BASE = 10000.0
HB = 8
PB = 64
WB = HB * PB
GB = 2
NB = 64
CONV_W = 3
CONV_CH = WB + 2 * GB * NB
HC = 4
DKC = 64
DVC = 64
WC = HC * DVC
D_MIX = WA + WB + WC
CHUNK = 128
N_EXPERTS = 32
TOP_K = 4
D_FF = D_MODEL
SWIGLU_LIMIT = 7.0
SWIGLU_ALPHA = 1.702
ALPHA = (2 * DEPTH) ** 0.25
LN_EPS = 1e-5

C_QA, C_KA, C_VA, C_OM = 0, 256, 512, 768
C_Z = 1024
C_XBC = 1536
C_QM, C_KM, C_VM = 2304, 2560, 2816
C_DTGM = 3072
N_PROJ = 3200
DT_LANES = 2 * HB
GM_LANE0 = DT_LANES

VMEM_LIMIT = 56 * 1024 * 1024
LANES = 128
SUBLANES = 8


def _cparams(*sem):
    return pltpu.CompilerParams(dimension_semantics=sem, vmem_limit_bytes=VMEM_LIMIT)


def _sigmoid(x):
    return 1.0 / (1.0 + jnp.exp(-x))


def _softplus(x):
    return jnp.maximum(x, 0.0) + jnp.log1p(jnp.exp(-jnp.abs(x)))


def _mod_kernel(c_ref, w_ref, b_ref, o_ref):
    c = c_ref[...]
    s = (c * _sigmoid(c)).astype(BF16)
    o_ref[...] = jnp.dot(s, w_ref[...].astype(BF16), preferred_element_type=F32) + b_ref[...]


def _modulation(cond, w_mod, b_mod):
    rows = cond.shape[0]
    n = w_mod.shape[1]
    tn = D_MODEL
    return pl.pallas_call(
        _mod_kernel,
        grid=(n // tn,),
        in_specs=[pl.BlockSpec((rows, D_MODEL), lambda j: (0, 0)),
                  pl.BlockSpec((D_MODEL, tn), lambda j: (0, j)),
                  pl.BlockSpec((1, tn), lambda j: (0, j))],
        out_specs=pl.BlockSpec((rows, tn), lambda j: (0, j)),
        out_shape=jax.ShapeDtypeStruct((rows, n), F32),
        compiler_params=_cparams("arbitrary"),
        name="modulation",
    )(cond, w_mod, b_mod.reshape(1, n))


def _inproj_kernel(x_ref, mod_ref, w_ref, cos_ref, sa_ref, sb_ref, o_ref, *, rope):
    x = x_ref[0]
    sh = mod_ref[0, 0:1, :]
    sc = mod_ref[0, 1:2, :]
    h = (x * (1.0 + sc) + sh).astype(BF16)
    p = jnp.dot(h, w_ref[...], preferred_element_type=F32)

    def rot(t):
        return (t * cos_ref[...] + pltpu.roll(t, WA - DQK // 4, 1) * sa_ref[...]
                + pltpu.roll(t, DQK // 4, 1) * sb_ref[...])

    q = p[:, C_QA:C_QA + WA]
    k = p[:, C_KA:C_KA + WA]
    if rope:
        q = rot(q)
        k = rot(k)
    o_ref[0, :, C_QA:C_QA + WA] = q * (DQK ** -0.5)
    o_ref[0, :, C_KA:C_KA + WA] = k
    o_ref[0, :, C_VA:] = p[:, C_VA:]


def _rope_tables(L):
    quarter = DQK // 4
    pos = np.arange(L)
    row = pos // GRID_W
    col = pos % GRID_W
    inv = ROPE_BASE ** (-np.arange(quarter, dtype=np.float32) / quarter)
    lane = np.arange(WA)
    c = lane % DQK
    use_col = (c // (DQK // 2)) == 1
    w = c % (DQK // 2)
    f = w % quarter
    first = w < quarter
    p = jnp.where(use_col[None, :], col[:, None], row[:, None]).astype(F32)
    ang = p * jnp.asarray(inv)[f][None, :]
    cos = jnp.cos(ang)
    sin = jnp.sin(ang)
    sa = jnp.where(first[None, :], -sin, 0.0)
    sb = jnp.where(first[None, :], 0.0, sin)
    return cos, sa, sb


def _inproj(x, mod, w_in_p, rope):
    B, L, _ = x.shape
    tl = min(L, 512)
    cos, sa, sb = _rope_tables(L)
    bm = mod.shape[0]
    mod_idx = (lambda b, i: (b, 0, 0)) if bm > 1 else (lambda b, i: (0, 0, 0))
    tab = pl.BlockSpec((tl, WA), lambda b, i: (i, 0))
    return pl.pallas_call(
        functools.partial(_inproj_kernel, rope=rope),
        grid=(B, L // tl),
        in_specs=[pl.BlockSpec((1, tl, D_MODEL), lambda b, i: (b, i, 0)),
                  pl.BlockSpec((1, 6, D_MODEL), mod_idx),
                  pl.BlockSpec((D_MODEL, N_PROJ), lambda b, i: (0, 0)),
                  tab, tab, tab],
        out_specs=pl.BlockSpec((1, tl, N_PROJ), lambda b, i: (b, i, 0)),
        out_shape=jax.ShapeDtypeStruct((B, L, N_PROJ), F32),
        compiler_params=_cparams("arbitrary", "arbitrary"),
        name="inproj",
    )(x, mod, w_in_p, cos, sa, sb)


def _permute_w_in(w_in):
    sizes = (WA, WA, WA, WB, CONV_CH, 2 * HB, WC, WC, WC, WC, 4 * HC)
    offs = np.concatenate([[0], np.cumsum(sizes)])
    qa, ka, va, z, xbc, dt, qm, km, vm, om, gm = (w_in[:, offs[i]:offs[i + 1]] for i in range(11))
    pad = jnp.zeros((w_in.shape[0], N_PROJ - C_DTGM - 2 * HB - 4 * HC), w_in.dtype)
    return jnp.concatenate([qa, ka, va, om, z, xbc, qm, km, vm, dt, gm, pad], axis=1).astype(BF16)


def _attn_kernel(lamp_ref, q_ref, kt_ref, v_ref, g_ref, o_ref, *, lam_init):
    lp = lamp_ref[...]
    lam = (jnp.exp(jnp.sum(lp[0:1] * lp[1:2], axis=-1, keepdims=True))
           - jnp.exp(jnp.sum(lp[2:3] * lp[3:4], axis=-1, keepdims=True)) + lam_init)
    q = q_ref[0]
    outs = []
    for h in range(HA):
        os_ = []
        for m in range(2):
            c0 = (2 * h + m) * DQK
            s = jnp.dot(q[:, c0:c0 + DQK].astype(BF16), kt_ref[0, c0:c0 + DQK, :],
                        preferred_element_type=F32)
            p = jnp.exp(s - jnp.max(s, axis=-1, keepdims=True)).astype(BF16)
            os_.append(jnp.dot(p, v_ref[0, h], preferred_element_type=F32))
        seg = (os_[0][:, :DVA] * (1.0 / os_[0][:, DVA:DVA + 1])
               - os_[1][:, :DVA] * (lam / os_[1][:, DVA:DVA + 1]))
        ms = jnp.mean(seg * seg, axis=-1, keepdims=True)
        outs.append(seg * lax.rsqrt(ms + LN_EPS))
    o_ref[0] = jnp.concatenate(outs, axis=1) * g_ref[...] * (1.0 - lam_init)


def _attention(proj, kt, v, lam_params, attn_g, layer):
    B, L, _ = proj.shape
    lk = kt.shape[2]
    tq = min(L, 256)
    lam_init = 0.8 - 0.6 * math.exp(-0.3 * layer)
    return pl.pallas_call(
        functools.partial(_attn_kernel, lam_init=lam_init),
        grid=(B, L // tq),
        in_specs=[pl.BlockSpec((4, DQK), lambda b, i: (0, 0)),
                  pl.BlockSpec((1, tq, WA), lambda b, i: (b, i, C_QA // WA)),
                  pl.BlockSpec((1, WA, lk), lambda b, i: (b, 0, 0)),
                  pl.BlockSpec((1, HA, lk, LANES), lambda b, i: (b, 0, 0, 0)),
                  pl.BlockSpec((1, WA), lambda b, i: (0, 0))],
        out_specs=pl.BlockSpec((1, tq, WA), lambda b, i: (b, i, 0)),
        out_shape=jax.ShapeDtypeStruct((B, L, WA), F32),
        compiler_params=_cparams("arbitrary", "arbitrary"),
        name="diff_attention",
    )(lam_params, proj, kt, v, jnp.tile(attn_g, HA).reshape(1, WA))


def _attn_operands(proj, ctx_k, ctx_v):
    B, L, _ = proj.shape
    k = proj[:, :, C_KA:C_KA + WA]
    v = proj[:, :, C_VA:C_VA + WA]
    if ctx_k is not None:
        k = jnp.concatenate([k, ctx_k.reshape(B, -1, WA)], axis=1)
        v = jnp.concatenate([v, ctx_v.reshape(B, -1, WA)], axis=1)
    lk = v.shape[1]
    vh = jnp.swapaxes(v.reshape(B, lk, HA, DVA), 1, 2).astype(BF16)
    vaug = jnp.concatenate([vh, jnp.ones((B, HA, lk, 1), BF16),
                            jnp.zeros((B, HA, lk, LANES - DVA - 1), BF16)], axis=-1)
    return jnp.swapaxes(k, 1, 2).astype(BF16), vaug


def _tri(rev):
    i = lax.broadcasted_iota(jnp.int32, (CHUNK, CHUNK), 0)
    j = lax.broadcasted_iota(jnp.int32, (CHUNK, CHUNK), 1)
    return (j >= i) if rev else (j <= i)


def _chunk_cumsum(a, rev):
    return jnp.dot(_tri(rev).astype(F32), a, precision=HIGHEST, preferred_element_type=F32)


def _dir_chunk(c, nc, d):
    return c if d == 0 else nc - 1 - c


def _ssd_dir(x_ref, xp_ref, xn_ref, dt_ref, cw_ref, cb_ref, dtb_ref, alog_ref, ht, chunk, nc, d):
    rev = d == 1
    x = x_ref[0]
    rowid = lax.broadcasted_iota(jnp.int32, (CHUNK, 1), 0)
    prev = jnp.where(chunk == 0, 0.0, xp_ref[0, SUBLANES - 1:SUBLANES, :])
    nxt = jnp.where(chunk == nc - 1, 0.0, xn_ref[0, 0:1, :])
    xm1 = jnp.where(rowid == 0, prev, pltpu.roll(x, 1, 0))
    xp1 = jnp.where(rowid == CHUNK - 1, nxt, pltpu.roll(x, CHUNK - 1, 0))
    xc = cw_ref[0:1, :] * xm1 + cw_ref[1:2, :] * x + cw_ref[2:3, :] * xp1 + cb_ref[...]
    xc = xc * _sigmoid(xc)
    xs = xc[:, :WB]

    lane = lax.broadcasted_iota(jnp.int32, (1, LANES), 1)
    dt = _softplus(dt_ref[0] + dtb_ref[...])
    aneg = jnp.where(lane < DT_LANES, -jnp.exp(alog_ref[...]), 0.0)
    cum = _chunk_cumsum(dt * aneg, rev)
    cum_t = cum.T
    tot = cum[0:1, :] if rev else cum[CHUNK - 1:CHUNK, :]
    er = lax.broadcasted_iota(jnp.int32, (LANES, WB), 0)
    ec = lax.broadcasted_iota(jnp.int32, (LANES, WB), 1)
    expand = (er == d * HB + ec // PB).astype(F32)

    def widen(t):
        return jnp.dot(t, expand, precision=HIGHEST, preferred_element_type=F32)

    dt_w = widen(dt)
    e_w = widen(jnp.exp(cum))
    te_w = widen(jnp.exp(tot - cum))
    xin = xs * dt_w
    xw = (xin * te_w).astype(BF16)
    xin_b = xin.astype(BF16)
    mask = _tri(rev)
    ht_b = ht.astype(BF16)
    ys, sts = [], []
    gw = WB // GB
    for g in range(GB):
        bm = xc[:, WB + g * NB:WB + (g + 1) * NB].astype(BF16)
        cm = xc[:, WB + GB * NB + g * NB:WB + GB * NB + (g + 1) * NB].astype(BF16)
        cb = lax.dot_general(cm, bm, (((1,), (1,)), ((), ())), preferred_element_type=F32)
        yd = []
        for r in range(HB // GB):
            h = g * (HB // GB) + r
            hd = d * HB + h
            seg = cum[:, hd:hd + 1] - cum_t[hd:hd + 1, :]
            dec = jnp.exp(jnp.where(mask, seg, -jnp.inf))
            yd.append(jnp.dot((cb * dec).astype(BF16), xin_b[:, h * PB:(h + 1) * PB],
                              preferred_element_type=F32))
        y_off = jnp.dot(cm, ht_b[:, g * gw:(g + 1) * gw], preferred_element_type=F32)
        ys.append(jnp.concatenate(yd, axis=1) + y_off * e_w[:, g * gw:(g + 1) * gw])
        sts.append(lax.dot_general(bm, xw[:, g * gw:(g + 1) * gw], (((0,), (0,)), ((), ())),
                                   preferred_element_type=F32))
    y = jnp.concatenate(ys, axis=1)
    cd = e_w[0:1, :] if rev else e_w[CHUNK - 1:CHUNK, :]
    ht_new = ht * cd + jnp.concatenate(sts, axis=1)
    return y, xs, ht_new


def _ssd_kernel(xf_ref, xfp_ref, xfn_ref, dtf_ref, xb_ref, xbp_ref, xbn_ref, dtb_ref,
                cw_ref, cb_ref, dtbias_ref, alog_ref, dskip_ref, h0_ref,
                yf_ref, yb_ref, hfin_ref, h_scr, *, nc):
    c = pl.program_id(1)

    @pl.when(c == 0)
    def _():
        h_scr[...] = h0_ref[0]

    yf, xs_f, hf = _ssd_dir(xf_ref, xfp_ref, xfn_ref, dtf_ref, cw_ref, cb_ref, dtbias_ref, alog_ref,
                            h_scr[0], c, nc, 0)
    yb, _, hb = _ssd_dir(xb_ref, xbp_ref, xbn_ref, dtb_ref, cw_ref, cb_ref, dtbias_ref, alog_ref,
                         h_scr[1], nc - 1 - c, nc, 1)
    yf_ref[0] = yf + xs_f * dskip_ref[...]
    yb_ref[0] = yb
    h_scr[0] = hf
    h_scr[1] = hb

    @pl.when(c == nc - 1)
    def _():
        hfin_ref[0] = h_scr[...]


def _pad_row(v, width=LANES):
    v = v.reshape(1, -1).astype(F32)
    return jnp.pad(v, ((0, 0), (0, width - v.shape[1])))


def _ssd(proj, conv_w, conv_b, dt_bias, a_log, d_skip, h0):
    B, L, _ = proj.shape
    nc = L // CHUNK
    r8 = CHUNK // SUBLANES
    nb8 = L // SUBLANES
    xblk = C_XBC // CONV_CH
    dblk = C_DTGM // LANES

    def specs(d):
        ch = lambda c: _dir_chunk(c, nc, d)
        return [
            pl.BlockSpec((1, CHUNK, CONV_CH), lambda b, c: (b, ch(c), xblk)),
            pl.BlockSpec((1, SUBLANES, CONV_CH), lambda b, c: (b, jnp.maximum(ch(c) * r8 - 1, 0), xblk)),
            pl.BlockSpec((1, SUBLANES, CONV_CH), lambda b, c: (b, jnp.minimum((ch(c) + 1) * r8, nb8 - 1), xblk)),
            pl.BlockSpec((1, CHUNK, LANES), lambda b, c: (b, ch(c), dblk)),
        ]

    const = lambda shape: pl.BlockSpec(shape, lambda b, c: (0,) * len(shape))
    state = pl.BlockSpec((1, 2, NB, WB), lambda b, c: (b, 0, 0, 0))
    return pl.pallas_call(
        functools.partial(_ssd_kernel, nc=nc),
        grid=(B, nc),
        in_specs=specs(0) + specs(1) + [const((CONV_W, CONV_CH)), const((1, CONV_CH)), const((1, LANES)),
                                        const((1, LANES)), const((1, WB)), state],
        out_specs=[pl.BlockSpec((1, CHUNK, WB), lambda b, c: (b, c, 0)),
                   pl.BlockSpec((1, CHUNK, WB), lambda b, c: (b, nc - 1 - c, 0)),
                   state],
        out_shape=[jax.ShapeDtypeStruct((B, L, WB), F32), jax.ShapeDtypeStruct((B, L, WB), F32),
                   jax.ShapeDtypeStruct((B, 2, NB, WB), F32)],
        scratch_shapes=[pltpu.VMEM((2, NB, WB), F32)],
        compiler_params=_cparams("arbitrary", "arbitrary"),
        name="ssd_scan",
    )(proj, proj, proj, proj, proj, proj, proj, proj,
      conv_w, conv_b.reshape(1, CONV_CH), _pad_row(dt_bias), _pad_row(a_log),
      jnp.repeat(d_skip, PB).reshape(1, WB), h0)


def _ssd_state_in(state):
    B = state.shape[0]
    return jnp.transpose(state, (0, 1, 4, 2, 3)).reshape(B, 2, NB, WB)


def _ssd_state_out(ht):
    B = ht.shape[0]
    return jnp.transpose(ht.reshape(B, 2, NB, HB, PB), (0, 1, 3, 4, 2))


CAUG = LANES


def _mlstm_dir(q_ref, k_ref, v_ref, g_ref, gbias_ref, caug, m_row, d):
    rev = d == 1
    q = q_ref[0]
    k = k_ref[0] * (DKC ** -0.5)
    v = v_ref[0]
    pre = g_ref[0] + gbias_ref[...]
    bc = _chunk_cumsum(-_softplus(-pre), rev)
    bc_t = bc.T
    pre_t = pre.T
    mask = _tri(rev)
    lane = lax.broadcasted_iota(jnp.int32, (1, LANES), 1)
    hs, cnew = [], []
    m_new_row = m_row
    for h in range(HC):
        li = GM_LANE0 + d * 2 * HC + h
        lf = li + HC
        bcol = bc[:, lf:lf + 1]
        tot = bcol[0:1, :] if rev else bcol[CHUNK - 1:CHUNK, :]
        dm = jnp.where(mask, bcol - bc_t[lf:lf + 1, :] + pre_t[li:li + 1, :], -jnp.inf)
        gend = tot - bcol + pre[:, li:li + 1]
        mloc = jnp.max(gend, axis=0, keepdims=True)
        kh = k[:, h * DKC:(h + 1) * DKC]
        kw = (kh * jnp.exp(gend - mloc)).astype(BF16)
        vt = v[:, (h // 2) * LANES:(h // 2 + 1) * LANES]
        if h % 2 == 1:
            vt = pltpu.roll(vt, DVC, 1)
        vaug = jnp.where(lane < DVC, vt, (lane == DVC).astype(F32)).astype(BF16)
        cn_loc = lax.dot_general(kw, vaug, (((0,), (0,)), ((), ())), preferred_element_type=F32)
        m_prev = m_row[:, h:h + 1]
        m_new = jnp.maximum(tot + m_prev, mloc)
        c_prev = caug[:, h * CAUG:(h + 1) * CAUG]
        cnew.append(jnp.exp(tot + m_prev - m_new) * c_prev + jnp.exp(mloc - m_new) * cn_loc)
        m_new_row = jnp.where(lane == h, m_new, m_new_row)
        inter = bcol + m_prev
        mt = jnp.maximum(inter, jnp.max(dm, axis=-1, keepdims=True))
        w_inter = jnp.exp(inter - mt)
        qh = q[:, h * DKC:(h + 1) * DKC].astype(BF16)
        qk = lax.dot_general(qh, kh.astype(BF16), (((1,), (1,)), ((), ())),
                             preferred_element_type=F32) * jnp.exp(dm - mt)
        num_in = jnp.dot(qk.astype(BF16), v[:, h * DVC:(h + 1) * DVC].astype(BF16), preferred_element_type=F32)
        den_in = jnp.sum(qk, axis=-1, keepdims=True)
        cross = jnp.dot(qh, c_prev.astype(BF16), preferred_element_type=F32)
        num = num_in + w_inter * cross[:, :DVC]
        den = den_in + w_inter * cross[:, DVC:DVC + 1]
        hs.append(num / jnp.maximum(jnp.abs(den), jnp.exp(-mt)))
    return jnp.concatenate(hs, axis=1), jnp.concatenate(cnew, axis=1), m_new_row


def _mlstm_kernel(qf_ref, kf_ref, vf_ref, gf_ref, qb_ref, kb_ref, vb_ref, gb_ref, gbias_ref, c0_ref, m0_ref,
                  hf_ref, hb_ref, cfin_ref, mfin_ref, c_scr, m_scr, *, nc):
    c = pl.program_id(1)

    @pl.when(c == 0)
    def _():
        c_scr[...] = c0_ref[0]
        m_scr[...] = m0_ref[0]

    hf, cf, mf = _mlstm_dir(qf_ref, kf_ref, vf_ref, gf_ref, gbias_ref, c_scr[0], m_scr[0], 0)
    hb, cb, mb = _mlstm_dir(qb_ref, kb_ref, vb_ref, gb_ref, gbias_ref, c_scr[1], m_scr[1], 1)
    hf_ref[0] = hf
    hb_ref[0] = hb
    c_scr[0] = cf
    c_scr[1] = cb
    m_scr[0] = mf
    m_scr[1] = mb

    @pl.when(c == nc - 1)
    def _():
        cfin_ref[0] = c_scr[...]
        mfin_ref[0] = m_scr[...]


def _mlstm(proj, ig_b, fg_b, c0, m0):
    B, L, _ = proj.shape
    nc = L // CHUNK
    gbias = jnp.concatenate([ig_b[0], fg_b[0], ig_b[1], fg_b[1]]).astype(F32)
    gbias = jnp.pad(gbias, (GM_LANE0, LANES - GM_LANE0 - 4 * HC)).reshape(1, LANES)

    def specs(d):
        ch = lambda c: _dir_chunk(c, nc, d)
        col = lambda off: pl.BlockSpec((1, CHUNK, WC), lambda b, c: (b, ch(c), off // WC))
        return [col(C_QM), col(C_KM), col(C_VM),
                pl.BlockSpec((1, CHUNK, LANES), lambda b, c: (b, ch(c), C_DTGM // LANES))]

    cstate = pl.BlockSpec((1, 2, DKC, HC * CAUG), lambda b, c: (b, 0, 0, 0))
    mstate = pl.BlockSpec((1, 2, 1, LANES), lambda b, c: (b, 0, 0, 0))
    return pl.pallas_call(
        functools.partial(_mlstm_kernel, nc=nc),
        grid=(B, nc),
        in_specs=specs(0) + specs(1) + [pl.BlockSpec((1, LANES), lambda b, c: (0, 0)), cstate, mstate],
        out_specs=[pl.BlockSpec((1, CHUNK, WC), lambda b, c: (b, c, 0)),
                   pl.BlockSpec((1, CHUNK, WC), lambda b, c: (b, nc - 1 - c, 0)),
                   cstate, mstate],
        out_shape=[jax.ShapeDtypeStruct((B, L, WC), F32), jax.ShapeDtypeStruct((B, L, WC), F32),
                   jax.ShapeDtypeStruct((B, 2, DKC, HC * CAUG), F32),
                   jax.ShapeDtypeStruct((B, 2, 1, LANES), F32)],
        scratch_shapes=[pltpu.VMEM((2, DKC, HC * CAUG), F32), pltpu.VMEM((2, 1, LANES), F32)],
        compiler_params=_cparams("arbitrary", "arbitrary"),
        name="mlstm_scan",
    )(proj, proj, proj, proj, proj, proj, proj, proj, gbias, c0, m0)


def _mlstm_state_in(C, n, m):
    B = C.shape[0]
    blk = jnp.concatenate([C, n[..., None], jnp.zeros(C.shape[:-1] + (CAUG - DVC - 1,), F32)], axis=-1)
    c0 = jnp.transpose(blk, (0, 1, 3, 2, 4)).reshape(B, 2, DKC, HC * CAUG)
    m0 = jnp.pad(m, ((0, 0), (0, 0), (0, LANES - HC))).reshape(B, 2, 1, LANES)
    return c0, m0


def _mlstm_state_out(cfin, mfin):
    B = cfin.shape[0]
    blk = jnp.transpose(cfin.reshape(B, 2, DKC, HC, CAUG), (0, 1, 3, 2, 4))
    return blk[..., :DVC], blk[..., DVC], mfin[:, :, 0, :HC]


RT_IDX0 = 0
RT_GATE0 = TOP_K
RT_ROWS = 2 * TOP_K


def _layer_norm_rows(x, g, b):
    mu = jnp.mean(x, axis=-1, keepdims=True)
    xc = x - mu
    var = jnp.mean(xc * xc, axis=-1, keepdims=True)
    return xc * lax.rsqrt(var + LN_EPS) * g + b


def _outproj_kernel(oa_ref, yf_ref, yb_ref, z_ref, hf_ref, hb_ref, om_ref, x_ref, mod_ref, w_ref,
                    sg_ref, mg_ref, lg_ref, lb_ref, rw_ref, rb_ref, x1_ref, h2_ref, rt_ref):
    z = z_ref[0]
    y = (yf_ref[0] + yb_ref[0]) * (z * _sigmoid(z))
    gw = WB // GB
    ob = []
    for g in range(GB):
        seg = y[:, g * gw:(g + 1) * gw]
        ob.append(seg * lax.rsqrt(jnp.mean(seg * seg, axis=-1, keepdims=True) + LN_EPS))
    ob = jnp.concatenate(ob, axis=1) * sg_ref[...]
    hsum = hf_ref[0] + hb_ref[0]
    oc = []
    for h in range(HC):
        seg = hsum[:, h * DVC:(h + 1) * DVC]
        mu = jnp.mean(seg, axis=-1, keepdims=True)
        sc = seg - mu
        oc.append(sc * lax.rsqrt(jnp.mean(sc * sc, axis=-1, keepdims=True) + LN_EPS))
    oc = jnp.concatenate(oc, axis=1) * mg_ref[...] * _sigmoid(om_ref[0])
    mix_in = jnp.concatenate([oa_ref[0], ob, oc], axis=1).astype(BF16)
    mix = jnp.dot(mix_in, w_ref[...], preferred_element_type=F32)
    x1 = _layer_norm_rows(ALPHA * x_ref[0] + mod_ref[0, 2:3, :] * mix, lg_ref[...], lb_ref[...])
    x1_ref[0] = x1
    h2 = x1 * (1.0 + mod_ref[0, 4:5, :]) + mod_ref[0, 3:4, :]
    for j in range(ROW_CH):
        h2_ref[pl.ds(j, h2.shape[0], stride=ROW_CH), :] = h2[:, j * LANES:(j + 1) * LANES]
    logits = jnp.dot(h2, rw_ref[...], precision=HIGHEST, preferred_element_type=F32) + rb_ref[...]
    lane = lax.broadcasted_iota(jnp.int32, (1, LANES), 1)
    lane_f = lane.astype(F32)
    lg = jnp.where(lane < N_EXPERTS, logits, -jnp.inf)
    vals, ids = [], []
    for _ in range(TOP_K):
        mx = jnp.max(lg, axis=-1, keepdims=True)
        first = jnp.min(jnp.where(lg == mx, lane_f, float(LANES)), axis=-1, keepdims=True)
        vals.append(mx)
        ids.append(first)
        lg = jnp.where(lane_f == first, -jnp.inf, lg)
    es = [jnp.exp(v - vals[0]) for v in vals]
    inv = 1.0 / (es[0] + es[1] + es[2] + es[3])
    rt = jnp.zeros(logits.shape, F32)
    for k in range(TOP_K):
        rt = jnp.where(lane == RT_IDX0 + k, ids[k], rt)
        rt = jnp.where(lane == RT_GATE0 + k, es[k] * inv, rt)
    rt_ref[...] = rt.T[0:RT_ROWS, :]


def _outproj(oa, yf, yb, hf, hb, proj, x, mod, w_out_b, ssd_g, mlstm_g, ln_g, ln_b, router_w, router_b):
    B, L, _ = x.shape
    tl = min(L, 256)
    bm = mod.shape[0]
    mod_idx = (lambda b, i: (b, 0, 0)) if bm > 1 else (lambda b, i: (0, 0, 0))
    row = lambda w: pl.BlockSpec((1, tl, w), lambda b, i: (b, i, 0))
    const = lambda shape: pl.BlockSpec(shape, lambda b, i: (0,) * len(shape))
    rw = jnp.pad(router_w.astype(F32), ((0, 0), (0, LANES - N_EXPERTS)))
    return pl.pallas_call(
        _outproj_kernel,
        grid=(B, L // tl),
        in_specs=[row(WA), row(WB), row(WB),
                  pl.BlockSpec((1, tl, WB), lambda b, i: (b, i, C_Z // WB)),
                  row(WC), row(WC),
                  pl.BlockSpec((1, tl, WC), lambda b, i: (b, i, C_OM // WC)),
                  row(D_MODEL),
                  pl.BlockSpec((1, 6, D_MODEL), mod_idx),
                  const((D_MIX, D_MODEL)), const((1, WB)), const((1, WC)),
                  const((1, D_MODEL)), const((1, D_MODEL)), const((D_MODEL, LANES)), const((1, LANES))],
        out_specs=[row(D_MODEL),
                   pl.BlockSpec((tl * ROW_CH, LANES), lambda b, i: (b * (L // tl) + i, 0)),
                   pl.BlockSpec((RT_ROWS, tl), lambda b, i: (0, b * (L // tl) + i))],
        out_shape=[jax.ShapeDtypeStruct((B, L, D_MODEL), F32),
                   jax.ShapeDtypeStruct((B * L * ROW_CH, LANES), F32),
                   jax.ShapeDtypeStruct((RT_ROWS, B * L), F32)],
        compiler_params=_cparams("arbitrary", "arbitrary"),
        name="outproj_ln_route",
    )(oa, yf, yb, proj, hf, hb, proj, x, mod, w_out_b, ssd_g.reshape(1, WB), mlstm_g.reshape(1, WC),
      ln_g.reshape(1, D_MODEL), ln_b.reshape(1, D_MODEL), rw, _pad_row(router_b))


def _final_ln_kernel(x_ref, ff_ref, mod_ref, g_ref, b_ref, o_ref):
    tl = x_ref.shape[1]
    ff = jnp.concatenate([ff_ref[pl.ds(j, tl, stride=ROW_CH), :] for j in range(ROW_CH)], axis=1)
    o_ref[0] = _layer_norm_rows(ALPHA * x_ref[0] + mod_ref[0, 5:6, :] * ff, g_ref[...], b_ref[...])


def _final_ln(x1, ff, mod, ln_g, ln_b):
    B, L, _ = x1.shape
    tl = min(L, 512)
    bm = mod.shape[0]
    mod_idx = (lambda b, i: (b, 0, 0)) if bm > 1 else (lambda b, i: (0, 0, 0))
    row = pl.BlockSpec((1, tl, D_MODEL), lambda b, i: (b, i, 0))
    const = pl.BlockSpec((1, D_MODEL), lambda b, i: (0, 0))
    return pl.pallas_call(
        _final_ln_kernel,
        grid=(B, L // tl),
        in_specs=[row, pl.BlockSpec((tl * ROW_CH, LANES), lambda b, i: (b * (L // tl) + i, 0)),
                  pl.BlockSpec((1, 6, D_MODEL), mod_idx), const, const],
        out_specs=row,
        out_shape=jax.ShapeDtypeStruct((B, L, D_MODEL), F32),
        compiler_params=_cparams("arbitrary", "arbitrary"),
        name="final_ln",
    )(x1, ff, mod, ln_g.reshape(1, D_MODEL), ln_b.reshape(1, D_MODEL))


MOE_TM = 2048
MOE_R = 288
MOE_U = SUBLANES
MOE_S = MOE_R + 1
MOE_RB = 512
ROW_CH = D_MODEL // LANES
MOE_LMAX = MOE_TM * TOP_K + N_EXPERTS * MOE_U


def _ceil_to_unit(x):
    return jnp.floor((x + (MOE_U - 1.0)) * (1.0 / MOE_U)) * float(MOE_U)


def _build_routing(rt_ref, d_vm, d_sm, g_sm, idx_sm, gate_sm, sems):
    r = rt_ref[...]
    eio = lax.broadcasted_iota(jnp.int32, (LANES, MOE_TM), 0).astype(F32)
    hit = [r[k:k + 1, :] == eio for k in range(TOP_K)]
    sel = jnp.zeros((LANES, MOE_TM), F32)
    for k in range(TOP_K):
        sel = jnp.where(hit[k], 1.0, sel)
    sel_b = sel.astype(BF16)
    tr = lax.broadcasted_iota(jnp.int32, (MOE_RB, MOE_RB), 0)
    tc = lax.broadcasted_iota(jnp.int32, (MOE_RB, MOE_RB), 1)
    before = (tr < tc).astype(BF16)
    run = jnp.zeros((LANES, 1), F32)
    ranks = []
    for c in range(MOE_TM // MOE_RB):
        blk = slice(c * MOE_RB, (c + 1) * MOE_RB)
        ranks.append(jnp.dot(sel_b[:, blk], before, preferred_element_type=F32) + run)
        run = run + jnp.sum(sel[:, blk], axis=1, keepdims=True)
    rank = jnp.concatenate(ranks, axis=1)
    pc_col = _ceil_to_unit(run)
    er = lax.broadcasted_iota(jnp.int32, (LANES, LANES), 0)
    ec = lax.broadcasted_iota(jnp.int32, (LANES, LANES), 1)
    offs_col = jnp.dot((ec < er).astype(F32), jnp.broadcast_to(pc_col, (LANES, LANES)),
                       precision=HIGHEST, preferred_element_type=F32)[:, 0:1]
    cnt_row = lax.dot_general(jnp.ones((SUBLANES, MOE_TM), BF16), sel_b, (((1,), (1,)), ((), ())),
                              preferred_element_type=F32)
    offs_row = jnp.dot(_ceil_to_unit(cnt_row), (er < ec).astype(F32), precision=HIGHEST,
                       preferred_element_type=F32)
    dest = offs_col + rank
    dest4 = [jnp.sum(jnp.where(hit[k], dest, 0.0), axis=0, keepdims=True) for k in range(TOP_K)]
    tail = jnp.concatenate([offs_row[0:RT_ROWS - TOP_K, :],
                            jnp.zeros((RT_ROWS - TOP_K, MOE_TM - LANES), F32)], axis=1)
    d_vm[...] = jnp.concatenate(dest4 + [tail], axis=0).astype(jnp.int32)
    cp_d = pltpu.make_async_copy(d_vm, d_sm, sems.at[0])
    cp_g = pltpu.make_async_copy(rt_ref, g_sm, sems.at[1])
    cp_d.start()
    cp_g.start()
    cp_d.wait()
    cp_g.wait()

    def pad_body(ex, c):
        end = d_sm[TOP_K, ex + 1]
        for u in range(MOE_U):
            pos = jnp.maximum(end - 1 - u, 0)
            idx_sm[pos] = MOE_TM
            gate_sm[pos] = 0.0
        return c

    lax.fori_loop(0, N_EXPERTS, pad_body, 0)

    def place_body(t, c):
        for k in range(TOP_K):
            pos = d_sm[k, t]
            idx_sm[pos] = t
            gate_sm[pos] = g_sm[RT_GATE0 + k, t]
        return c

    lax.fori_loop(0, MOE_TM, place_body, 0)


def _moe_kernel(rt_ref, x_ref, wgu_ref, bgu_ref, wdn_ref, bdn_ref, o_hbm,
                acc, xt, yt, d_vm, d_sm, g_sm, idx_sm, gate_sm, sems):
    i = pl.program_id(0)
    e = pl.program_id(1)

    @pl.when((i == 0) & (e == 0))
    def _():
        xt[...] = jnp.zeros(xt.shape, F32)

    @pl.when(e == 0)
    def _():
        acc[...] = jnp.zeros(acc.shape, F32)
        _build_routing(rt_ref, d_vm, d_sm, g_sm, idx_sm, gate_sm, sems)

    base = d_sm[TOP_K, e]
    cnt = d_sm[TOP_K, e + 1] - base

    def chunk_body(cix, carry):
        r0 = base + cix * MOE_R
        ngrp = lax.shift_right_logical(jnp.minimum(MOE_R, cnt - cix * MOE_R), 3)

        def gather(gi, c2):
            for u in range(MOE_U):
                mi = gi * MOE_U + u
                t = jnp.minimum(idx_sm[r0 + mi], MOE_TM - 1)
                src = pl.multiple_of(t * ROW_CH, ROW_CH)
                xt[pl.ds(mi, ROW_CH, stride=MOE_S), :] = x_ref[pl.ds(src, ROW_CH), :]
            return c2

        lax.fori_loop(0, ngrp, gather, 0)
        x = jnp.concatenate([xt[pl.ds(j * MOE_S, MOE_R), :] for j in range(ROW_CH)], axis=1).astype(BF16)
        gu = jnp.dot(x, wgu_ref[0], preferred_element_type=F32) + bgu_ref[0]
        g = jnp.minimum(gu[:, :D_FF], SWIGLU_LIMIT)
        u_ = jnp.clip(gu[:, D_FF:], -SWIGLU_LIMIT, SWIGLU_LIMIT)
        act = g * _sigmoid(SWIGLU_ALPHA * g) * (u_ + 1.0)
        y = jnp.dot(act.astype(BF16), wdn_ref[0], preferred_element_type=F32) + bdn_ref[0]
        for j in range(ROW_CH):
            yt[pl.ds(j * MOE_S, MOE_R), :] = y[:, j * LANES:(j + 1) * LANES]

        def scatter(gi, c2):
            upd = []
            for u in range(MOE_U):
                mi = gi * MOE_U + u
                dst = pl.multiple_of(idx_sm[r0 + mi] * ROW_CH, ROW_CH)
                gt = gate_sm[r0 + mi]
                upd.append((dst, acc[pl.ds(dst, ROW_CH), :] + gt * yt[pl.ds(mi, ROW_CH, stride=MOE_S), :]))
            for dst, val in upd:
                acc[pl.ds(dst, ROW_CH), :] = val
            return c2

        lax.fori_loop(0, ngrp, scatter, 0)
        return carry

    lax.fori_loop(0, lax.div(cnt + (MOE_R - 1), MOE_R), chunk_body, 0)

    @pl.when(e == N_EXPERTS - 1)
    def _():
        rows = MOE_TM * ROW_CH
        cp = pltpu.make_async_copy(acc.at[pl.ds(0, rows), :],
                                   o_hbm.at[pl.ds(pl.multiple_of(i * rows, rows), rows), :], sems.at[2])
        cp.start()
        cp.wait()


def _moe(xv, rt, w_gu_b, b_gu, w_dn_b, b_dn):
    T = xv.shape[0] // ROW_CH
    nt = T // MOE_TM
    return pl.pallas_call(
        _moe_kernel,
        grid=(nt, N_EXPERTS),
        in_specs=[pl.BlockSpec((RT_ROWS, MOE_TM), lambda i, e: (0, i)),
                  pl.BlockSpec((MOE_TM * ROW_CH, LANES), lambda i, e: (i, 0)),
                  pl.BlockSpec((1, D_MODEL, 2 * D_FF), lambda i, e: (e, 0, 0)),
                  pl.BlockSpec((1, 1, 2 * D_FF), lambda i, e: (e, 0, 0)),
                  pl.BlockSpec((1, D_FF, D_MODEL), lambda i, e: (e, 0, 0)),
                  pl.BlockSpec((1, 1, D_MODEL), lambda i, e: (e, 0, 0))],
        out_specs=pl.BlockSpec(memory_space=pl.ANY),
        scratch_shapes=[pltpu.VMEM(((MOE_TM + 1) * ROW_CH, LANES), F32),
                        pltpu.VMEM((ROW_CH * MOE_S, LANES), F32),
                        pltpu.VMEM((ROW_CH * MOE_S, LANES), F32),
                        pltpu.VMEM((RT_ROWS, MOE_TM), jnp.int32),
                        pltpu.SMEM((RT_ROWS, MOE_TM), jnp.int32),
                        pltpu.SMEM((RT_ROWS, MOE_TM), F32),
                        pltpu.SMEM((MOE_LMAX,), jnp.int32),
                        pltpu.SMEM((MOE_LMAX,), F32),
                        pltpu.SemaphoreType.DMA((3,))],
        out_shape=jax.ShapeDtypeStruct((T * ROW_CH, LANES), F32),
        compiler_params=_cparams("arbitrary", "arbitrary"),
        name="moe_ffn",
    )(rt, xv, w_gu_b, b_gu.reshape(N_EXPERTS, 1, 2 * D_FF), w_dn_b, b_dn.reshape(N_EXPERTS, 1, D_MODEL))


def _trunk_layer(x, mod, p, layer, ctx):
    B, L, _ = x.shape
    proj = _inproj(x, mod, p['w_in'], rope=ctx is not None)
    kt, v = _attn_operands(proj, None if ctx is None else ctx['k'], None if ctx is None else ctx['v'])
    oa = _attention(proj, kt, v, p['lam'], p['attn_g'], layer)
    if ctx is None:
        h0 = jnp.zeros((B, 2, NB, WB), F32)
        c0 = jnp.zeros((B, 2, DKC, HC * CAUG), F32)
        m0 = jnp.zeros((B, 2, 1, LANES), F32)
    else:
        h0 = _ssd_state_in(ctx['ssd'])
        c0, m0 = _mlstm_state_in(ctx['C'], ctx['n'], ctx['m'])
    yf, yb, hfin = _ssd(proj, p['conv_w'], p['conv_b'], p['dt_bias'], p['a_log'], p['d_skip'], h0)
    hf, hb, cfin, mfin = _mlstm(proj, p['ig_b'], p['fg_b'], c0, m0)
    x1, h2, rt = _outproj(oa, yf, yb, hf, hb, proj, x, mod, p['w_out'], p['ssd_g'], p['mlstm_g'],
                          p['ln1_g'], p['ln1_b'], p['router_w'], p['router_b'])
    ff = _moe(h2, rt, p['w_gu'], p['b_gu'], p['w_dn'], p['b_dn'])
    x2 = _final_ln(x1, ff, mod, p['ln2_g'], p['ln2_b'])
    new_ctx = None
    if ctx is None:
        c_out, n_out, m_out = _mlstm_state_out(cfin, mfin)
        new_ctx = (proj[:, :, C_KA:C_KA + WA].reshape(B, L, HA, 2 * DQK),
                   proj[:, :, C_VA:C_VA + WA].reshape(B, L, HA, DVA),
                   _ssd_state_out(hfin), c_out, n_out, m_out)
    return x2, new_ctx


def kernel(x_prompt, x_sample, c, cache_attn_k, cache_attn_v, state_ssd, state_mlstm_C, state_mlstm_n, state_mlstm_m, c_ctx, w_mod, b_mod, w_in, lam_q1, lam_k1, lam_q2, lam_k2, attn_g, conv_w, conv_b, dt_bias, a_log, d_skip, ssd_g, ig_b, fg_b, mlstm_g, w_out, ln1_g, ln1_b, router_w, router_b, w_gu, b_gu, w_dn, b_dn, ln2_g, ln2_b):
    nb = c.shape[0]
    cond = jnp.concatenate([c_ctx[None, :], c, jnp.zeros((2 * SUBLANES - 1 - nb, D_MODEL), F32)], axis=0)
    y_prompt, y_sample = x_prompt, x_sample
    outs = [[] for _ in range(6)]
    for l in range(DEPTH):
        p = {'w_in': _permute_w_in(w_in[l]),
             'lam': jnp.stack([lam_q1[l], lam_k1[l], lam_q2[l], lam_k2[l]]),
             'attn_g': attn_g[l], 'conv_w': conv_w[l], 'conv_b': conv_b[l], 'dt_bias': dt_bias[l],
             'a_log': a_log[l], 'd_skip': d_skip[l], 'ssd_g': ssd_g[l], 'ig_b': ig_b[l], 'fg_b': fg_b[l],
             'mlstm_g': mlstm_g[l], 'w_out': w_out[l].astype(BF16), 'ln1_g': ln1_g[l], 'ln1_b': ln1_b[l],
             'router_w': router_w[l], 'router_b': router_b[l], 'w_gu': w_gu[l].astype(BF16), 'b_gu': b_gu[l],
             'w_dn': w_dn[l].astype(BF16), 'b_dn': b_dn[l], 'ln2_g': ln2_g[l], 'ln2_b': ln2_b[l]}
        mod = _modulation(cond, w_mod[l], b_mod[l]).reshape(2 * SUBLANES, 6, D_MODEL)
        y_prompt, st = _trunk_layer(y_prompt, mod[0:1], p, l, None)
        for acc_list, s in zip(outs, st):
            acc_list.append(s)
        ctx = {'k': cache_attn_k[:, l], 'v': cache_attn_v[:, l], 'ssd': state_ssd[:, l],
               'C': state_mlstm_C[:, l], 'n': state_mlstm_n[:, l], 'm': state_mlstm_m[:, l]}
        y_sample, _ = _trunk_layer(y_sample, mod[1:1 + nb], p, l, ctx)
    return (y_prompt, y_sample) + tuple(jnp.stack(o, axis=1) for o in outs)
```

```python
import functools
import math

import jax
import jax.numpy as jnp
import numpy as np
from jax import lax
from jax.experimental import pallas as pl
from jax.experimental.pallas import tpu as pltpu

F32 = jnp.float32
BF16 = jnp.bfloat16
HIGHEST = lax.Precision.HIGHEST

D_MODEL = 1024
DEPTH = 2
GRID_W = 64
HA = 4
DQK = 32
DVA = 2 * DQK
WA = HA * DVA
ROPE_BASE = 10000.0
HB = 8
PB = 64
WB = HB * PB
GB = 2
NB = 64
CONV_W = 3
CONV_CH = WB + 2 * GB * NB
HC = 4
DKC = 64
DVC = 64
WC = HC * DVC
D_MIX = WA + WB + WC
CHUNK = 128
N_EXPERTS = 32
TOP_K = 4
D_FF = D_MODEL
SWIGLU_LIMIT = 7.0
SWIGLU_ALPHA = 1.702
ALPHA = (2 * DEPTH) ** 0.25
LN_EPS = 1e-5

C_QA, C_KA, C_VA, C_OM = 0, 256, 512, 768
C_Z = 1024
C_XBC = 1536
C_QM, C_KM, C_VM = 2304, 2560, 2816
C_DTGM = 3072
N_PROJ = 3200
DT_LANES = 2 * HB
GM_LANE0 = DT_LANES

VMEM_LIMIT = 56 * 1024 * 1024
LANES = 128
SUBLANES = 8


def _cparams(*sem):
    return pltpu.CompilerParams(dimension_semantics=sem, vmem_limit_bytes=VMEM_LIMIT)


def _sigmoid(x):
    return 1.0 / (1.0 + jnp.exp(-x))


def _softplus(x):
    return jnp.maximum(x, 0.0) + jnp.log1p(jnp.exp(-jnp.abs(x)))


def _mod_kernel(c_ref, w_ref, b_ref, o_ref):
    c = c_ref[...]
    s = (c * _sigmoid(c)).astype(BF16)
    o_ref[...] = jnp.dot(s, w_ref[...].astype(BF16), preferred_element_type=F32) + b_ref[...]


def _modulation(cond, w_mod, b_mod):
    rows = cond.shape[0]
    n = w_mod.shape[1]
    tn = D_MODEL
    return pl.pallas_call(
        _mod_kernel,
        grid=(n // tn,),
        in_specs=[pl.BlockSpec((rows, D_MODEL), lambda j: (0, 0)),
                  pl.BlockSpec((D_MODEL, tn), lambda j: (0, j)),
                  pl.BlockSpec((1, tn), lambda j: (0, j))],
        out_specs=pl.BlockSpec((rows, tn), lambda j: (0, j)),
        out_shape=jax.ShapeDtypeStruct((rows, n), F32),
        compiler_params=_cparams("arbitrary"),
        name="modulation",
    )(cond, w_mod, b_mod.reshape(1, n))


def _inproj_kernel(x_ref, mod_ref, w_ref, cos_ref, sa_ref, sb_ref, o_ref, *, rope):
    x = x_ref[0]
    sh = mod_ref[0, 0:1, :]
    sc = mod_ref[0, 1:2, :]
    h = (x * (1.0 + sc) + sh).astype(BF16)
    p = jnp.dot(h, w_ref[...], preferred_element_type=F32)

    def rot(t):
        return (t * cos_ref[...] + pltpu.roll(t, WA - DQK // 4, 1) * sa_ref[...]
                + pltpu.roll(t, DQK // 4, 1) * sb_ref[...])

    q = p[:, C_QA:C_QA + WA]
    k = p[:, C_KA:C_KA + WA]
    if rope:
        q = rot(q)
        k = rot(k)
    o_ref[0, :, C_QA:C_QA + WA] = q * (DQK ** -0.5)
    o_ref[0, :, C_KA:C_KA + WA] = k
    o_ref[0, :, C_VA:] = p[:, C_VA:]


def _rope_tables(L):
    quarter = DQK // 4
    pos = np.arange(L)
    row = pos // GRID_W
    col = pos % GRID_W
    inv = ROPE_BASE ** (-np.arange(quarter, dtype=np.float32) / quarter)
    lane = np.arange(WA)
    c = lane % DQK
    use_col = (c // (DQK // 2)) == 1
    w = c % (DQK // 2)
    f = w % quarter
    first = w < quarter
    p = jnp.where(use_col[None, :], col[:, None], row[:, None]).astype(F32)
    ang = p * jnp.asarray(inv)[f][None, :]
    cos = jnp.cos(ang)
    sin = jnp.sin(ang)
    sa = jnp.where(first[None, :], -sin, 0.0)
    sb = jnp.where(first[None, :], 0.0, sin)
    return cos, sa, sb


def _inproj(x, mod, w_in_p, rope):
    B, L, _ = x.shape
    tl = min(L, 512)
    cos, sa, sb = _rope_tables(L)
    bm = mod.shape[0]
    mod_idx = (lambda b, i: (b, 0, 0)) if bm > 1 else (lambda b, i: (0, 0, 0))
    tab = pl.BlockSpec((tl, WA), lambda b, i: (i, 0))
    return pl.pallas_call(
        functools.partial(_inproj_kernel, rope=rope),
        grid=(B, L // tl),
        in_specs=[pl.BlockSpec((1, tl, D_MODEL), lambda b, i: (b, i, 0)),
                  pl.BlockSpec((1, 6, D_MODEL), mod_idx),
                  pl.BlockSpec((D_MODEL, N_PROJ), lambda b, i: (0, 0)),
                  tab, tab, tab],
        out_specs=pl.BlockSpec((1, tl, N_PROJ), lambda b, i: (b, i, 0)),
        out_shape=jax.ShapeDtypeStruct((B, L, N_PROJ), F32),
        compiler_params=_cparams("arbitrary", "arbitrary"),
        name="inproj",
    )(x, mod, w_in_p, cos, sa, sb)


def _permute_w_in(w_in):
    sizes = (WA, WA, WA, WB, CONV_CH, 2 * HB, WC, WC, WC, WC, 4 * HC)
    offs = np.concatenate([[0], np.cumsum(sizes)])
    qa, ka, va, z, xbc, dt, qm, km, vm, om, gm = (w_in[:, offs[i]:offs[i + 1]] for i in range(11))
    pad = jnp.zeros((w_in.shape[0], N_PROJ - C_DTGM - 2 * HB - 4 * HC), w_in.dtype)
    return jnp.concatenate([qa, ka, va, om, z, xbc, qm, km, vm, dt, gm, pad], axis=1).astype(BF16)


def _attn_kernel(lamp_ref, q_ref, kt_ref, v_ref, g_ref, o_ref, *, lam_init):
    lp = lamp_ref[...]
    lam = (jnp.exp(jnp.sum(lp[0:1] * lp[1:2], axis=-1, keepdims=True))
           - jnp.exp(jnp.sum(lp[2:3] * lp[3:4], axis=-1, keepdims=True)) + lam_init)
    q = q_ref[0]
    outs = []
    for h in range(HA):
        os_ = []
        for m in range(2):
            c0 = (2 * h + m) * DQK
            s = jnp.dot(q[:, c0:c0 + DQK].astype(BF16), kt_ref[0, c0:c0 + DQK, :],
                        preferred_element_type=F32)
            p = jnp.exp(s - jnp.max(s, axis=-1, keepdims=True)).astype(BF16)
            os_.append(jnp.dot(p, v_ref[0, h], preferred_element_type=F32))
        seg = (os_[0][:, :DVA] * (1.0 / os_[0][:, DVA:DVA + 1])
               - os_[1][:, :DVA] * (lam / os_[1][:, DVA:DVA + 1]))
        ms = jnp.mean(seg * seg, axis=-1, keepdims=True)
        outs.append(seg * lax.rsqrt(ms + LN_EPS))
    o_ref[0] = jnp.concatenate(outs, axis=1) * g_ref[...] * (1.0 - lam_init)


def _attention(proj, kt, v, lam_params, attn_g, layer):
    B, L, _ = proj.shape
    lk = kt.shape[2]
    tq = min(L, 256)
    lam_init = 0.8 - 0.6 * math.exp(-0.3 * layer)
    return pl.pallas_call(
        functools.partial(_attn_kernel, lam_init=lam_init),
        grid=(B, L // tq),
        in_specs=[pl.BlockSpec((4, DQK), lambda b, i: (0, 0)),
                  pl.BlockSpec((1, tq, WA), lambda b, i: (b, i, C_QA // WA)),
                  pl.BlockSpec((1, WA, lk), lambda b, i: (b, 0, 0)),
                  pl.BlockSpec((1, HA, lk, LANES), lambda b, i: (b, 0, 0, 0)),
                  pl.BlockSpec((1, WA), lambda b, i: (0, 0))],
        out_specs=pl.BlockSpec((1, tq, WA), lambda b, i: (b, i, 0)),
        out_shape=jax.ShapeDtypeStruct((B, L, WA), F32),
        compiler_params=_cparams("arbitrary", "arbitrary"),
        name="diff_attention",
    )(lam_params, proj, kt, v, jnp.tile(attn_g, HA).reshape(1, WA))


def _attn_operands(proj, ctx_k, ctx_v):
    B, L, _ = proj.shape
    k = proj[:, :, C_KA:C_KA + WA]
    v = proj[:, :, C_VA:C_VA + WA]
    if ctx_k is not None:
        k = jnp.concatenate([k, ctx_k.reshape(B, -1, WA)], axis=1)
        v = jnp.concatenate([v, ctx_v.reshape(B, -1, WA)], axis=1)
    lk = v.shape[1]
    vh = jnp.swapaxes(v.reshape(B, lk, HA, DVA), 1, 2).astype(BF16)
    vaug = jnp.concatenate([vh, jnp.ones((B, HA, lk, 1), BF16),
                            jnp.zeros((B, HA, lk, LANES - DVA - 1), BF16)], axis=-1)
    return jnp.swapaxes(k, 1, 2).astype(BF16), vaug


def _tri(rev):
    i = lax.broadcasted_iota(jnp.int32, (CHUNK, CHUNK), 0)
    j = lax.broadcasted_iota(jnp.int32, (CHUNK, CHUNK), 1)
    return (j >= i) if rev else (j <= i)


def _split3(x):
    hi = x.astype(BF16)
    r1 = x - hi.astype(F32)
    mid = r1.astype(BF16)
    lo = (r1 - mid.astype(F32)).astype(BF16)
    return hi, mid, lo


def _chunk_cumsum(a, rev):
    tri = _tri(rev).astype(BF16)
    return jnp.dot(jnp.concatenate([tri, tri, tri], axis=1), jnp.concatenate(_split3(a), axis=0),
                   preferred_element_type=F32)


def _widen(t, onehot3):
    return jnp.dot(jnp.concatenate(_split3(t), axis=1), onehot3, preferred_element_type=F32)


def _onehot3(src_lane_of_col, width):
    r = lax.broadcasted_iota(jnp.int32, (3 * LANES, width), 0) % LANES
    c = lax.broadcasted_iota(jnp.int32, (3 * LANES, width), 1)
    return (r == src_lane_of_col(c)).astype(BF16)


def _dir_chunk(c, nc, d):
    return c if d == 0 else nc - 1 - c


def _ssd_dir(x_ref, xp_ref, xn_ref, dt_ref, cw_ref, cb_ref, dtb_ref, alog_ref, ht, chunk, nc, d):
    rev = d == 1
    x = x_ref[0]
    rowid = lax.broadcasted_iota(jnp.int32, (CHUNK, 1), 0)
    prev = jnp.where(chunk == 0, 0.0, xp_ref[0, SUBLANES - 1:SUBLANES, :])
    nxt = jnp.where(chunk == nc - 1, 0.0, xn_ref[0, 0:1, :])
    xm1 = jnp.where(rowid == 0, prev, pltpu.roll(x, 1, 0))
    xp1 = jnp.where(rowid == CHUNK - 1, nxt, pltpu.roll(x, CHUNK - 1, 0))
    xc = cw_ref[0:1, :] * xm1 + cw_ref[1:2, :] * x + cw_ref[2:3, :] * xp1 + cb_ref[...]
    xc = xc * _sigmoid(xc)
    xs = xc[:, :WB]

    lane = lax.broadcasted_iota(jnp.int32, (1, LANES), 1)
    dt = _softplus(dt_ref[0] + dtb_ref[...])
    aneg = jnp.where(lane < DT_LANES, -jnp.exp(alog_ref[...]), 0.0)
    cum = _chunk_cumsum(dt * aneg, rev)
    cum_t = cum.T
    expand = _onehot3(lambda c: d * HB + c // PB, WB)
    dt_w = _widen(dt, expand)
    cum_w = _widen(cum, expand)
    tot_w = cum_w[0:1, :] if rev else cum_w[CHUNK - 1:CHUNK, :]
    e_w = jnp.exp(cum_w)
    te_w = jnp.exp(tot_w - cum_w)
    xin = xs * dt_w
    xw = (xin * te_w).astype(BF16)
    xin_b = xin.astype(BF16)
    mask = _tri(rev)
    ht_b = ht.astype(BF16)
    ys, sts = [], []
    gw = WB // GB
    for g in range(GB):
        bm = xc[:, WB + g * NB:WB + (g + 1) * NB].astype(BF16)
        cm = xc[:, WB + GB * NB + g * NB:WB + GB * NB + (g + 1) * NB].astype(BF16)
        cb = lax.dot_general(cm, bm, (((1,), (1,)), ((), ())), preferred_element_type=F32)
        yd = []
        for r in range(HB // GB):
            h = g * (HB // GB) + r
            hd = d * HB + h
            seg = cum[:, hd:hd + 1] - cum_t[hd:hd + 1, :]
            dec = jnp.exp(jnp.where(mask, seg, -jnp.inf))
            yd.append(jnp.dot((cb * dec).astype(BF16), xin_b[:, h * PB:(h + 1) * PB],
                              preferred_element_type=F32))
        y_off = jnp.dot(cm, ht_b[:, g * gw:(g + 1) * gw], preferred_element_type=F32)
        ys.append(jnp.concatenate(yd, axis=1) + y_off * e_w[:, g * gw:(g + 1) * gw])
        sts.append(lax.dot_general(bm, xw[:, g * gw:(g + 1) * gw], (((0,), (0,)), ((), ())),
                                   preferred_element_type=F32))
    y = jnp.concatenate(ys, axis=1)
    cd = e_w[0:1, :] if rev else e_w[CHUNK - 1:CHUNK, :]
    ht_new = ht * cd + jnp.concatenate(sts, axis=1)
    return y, xs, ht_new


def _ssd_kernel(xf_ref, xfp_ref, xfn_ref, dtf_ref, xb_ref, xbp_ref, xbn_ref, dtb_ref,
                cw_ref, cb_ref, dtbias_ref, alog_ref, dskip_ref, h0_ref,
                yf_ref, yb_ref, hfin_ref, h_scr, *, nc):
    c = pl.program_id(1)

    @pl.when(c == 0)
    def _():
        h_scr[...] = h0_ref[0]

    yf, xs_f, hf = _ssd_dir(xf_ref, xfp_ref, xfn_ref, dtf_ref, cw_ref, cb_ref, dtbias_ref, alog_ref,
                            h_scr[0], c, nc, 0)
    yb, _, hb = _ssd_dir(xb_ref, xbp_ref, xbn_ref, dtb_ref, cw_ref, cb_ref, dtbias_ref, alog_ref,
                         h_scr[1], nc - 1 - c, nc, 1)
    yf_ref[0] = yf + xs_f * dskip_ref[...]
    yb_ref[0] = yb
    h_scr[0] = hf
    h_scr[1] = hb

    @pl.when(c == nc - 1)
    def _():
        hfin_ref[0] = h_scr[...]


def _pad_row(v, width=LANES):
    v = v.reshape(1, -1).astype(F32)
    return jnp.pad(v, ((0, 0), (0, width - v.shape[1])))


def _ssd(proj, conv_w, conv_b, dt_bias, a_log, d_skip, h0):
    B, L, _ = proj.shape
    nc = L // CHUNK
    r8 = CHUNK // SUBLANES
    nb8 = L // SUBLANES
    xblk = C_XBC // CONV_CH
    dblk = C_DTGM // LANES

    def specs(d):
        ch = lambda c: _dir_chunk(c, nc, d)
        return [
            pl.BlockSpec((1, CHUNK, CONV_CH), lambda b, c: (b, ch(c), xblk)),
            pl.BlockSpec((1, SUBLANES, CONV_CH), lambda b, c: (b, jnp.maximum(ch(c) * r8 - 1, 0), xblk)),
            pl.BlockSpec((1, SUBLANES, CONV_CH), lambda b, c: (b, jnp.minimum((ch(c) + 1) * r8, nb8 - 1), xblk)),
            pl.BlockSpec((1, CHUNK, LANES), lambda b, c: (b, ch(c), dblk)),
        ]

    const = lambda shape: pl.BlockSpec(shape, lambda b, c: (0,) * len(shape))
    state = pl.BlockSpec((1, 2, NB, WB), lambda b, c: (b, 0, 0, 0))
    return pl.pallas_call(
        functools.partial(_ssd_kernel, nc=nc),
        grid=(B, nc),
        in_specs=specs(0) + specs(1) + [const((CONV_W, CONV_CH)), const((1, CONV_CH)), const((1, LANES)),
                                        const((1, LANES)), const((1, WB)), state],
        out_specs=[pl.BlockSpec((1, CHUNK, WB), lambda b, c: (b, c, 0)),
                   pl.BlockSpec((1, CHUNK, WB), lambda b, c: (b, nc - 1 - c, 0)),
                   state],
        out_shape=[jax.ShapeDtypeStruct((B, L, WB), F32), jax.ShapeDtypeStruct((B, L, WB), F32),
                   jax.ShapeDtypeStruct((B, 2, NB, WB), F32)],
        scratch_shapes=[pltpu.VMEM((2, NB, WB), F32)],
        compiler_params=_cparams("arbitrary", "arbitrary"),
        name="ssd_scan",
    )(proj, proj, proj, proj, proj, proj, proj, proj,
      conv_w, conv_b.reshape(1, CONV_CH), _pad_row(dt_bias), _pad_row(a_log),
      jnp.repeat(d_skip, PB).reshape(1, WB), h0)


def _ssd_state_in(state):
    B = state.shape[0]
    return jnp.transpose(state, (0, 1, 4, 2, 3)).reshape(B, 2, NB, WB)


def _ssd_state_out(ht):
    B = ht.shape[0]
    return jnp.transpose(ht.reshape(B, 2, NB, HB, PB), (0, 1, 3, 4, 2))


def _fg_lane(d, h):
    return GM_LANE0 + d * 2 * HC + HC + h


def _mlstm_dir(q_ref, k_ref, v_ref, g_ref, gbias_ref, cbd, nmat, m_prev, d):
    rev = d == 1
    q_b = q_ref[0].astype(BF16)
    k = k_ref[0] * (DKC ** -0.5)
    k_b = k.astype(BF16)
    v_b = v_ref[0].astype(BF16)
    pre = g_ref[0] + gbias_ref[...]
    bc = _chunk_cumsum(-_softplus(-pre), rev)
    ig = pltpu.roll(pre, HC, 1)
    tot = bc[0:1, :] if rev else bc[CHUNK - 1:CHUNK, :]
    gend = tot - bc + ig
    mloc = jnp.max(gend, axis=0, keepdims=True)
    wend = jnp.exp(gend - mloc)
    m_new = jnp.maximum(tot + m_prev, mloc)
    a_prev = jnp.exp(tot + m_prev - m_new)
    a_loc = jnp.exp(mloc - m_new)
    inter = bc + m_prev
    bc_t = bc.T
    ig_t = ig.T
    lf0 = _fg_lane(d, 0)
    to_tile = _onehot3(lambda c: lf0 + c // LANES, HC * LANES)
    to_head = _onehot3(lambda c: lf0 + c // DVC, WC)
    lane = lax.broadcasted_iota(jnp.int32, (1, LANES), 1)
    mask = _tri(rev)
    bc_w = _widen(bc, to_tile)
    dms = []
    rmax = jnp.zeros((CHUNK, LANES), F32)
    for h in range(HC):
        lf = lf0 + h
        dm = jnp.where(mask, bc_w[:, h * LANES:(h + 1) * LANES] - bc_t[lf:lf + 1, :] + ig_t[lf:lf + 1, :],
                       -jnp.inf)
        dms.append(dm)
        rmax = jnp.where(lane == lf, jnp.max(dm, axis=-1, keepdims=True), rmax)
    mt = jnp.maximum(inter, rmax)
    mt_w = _widen(mt, to_tile)
    w_inter = jnp.exp(inter - mt)
    nums = []
    den_in = jnp.zeros((CHUNK, LANES), F32)
    for h in range(HC):
        hk = slice(h * DKC, (h + 1) * DKC)
        qk = lax.dot_general(q_b[:, hk], k_b[:, hk], (((1,), (1,)), ((), ())), preferred_element_type=F32)
        qk = qk * jnp.exp(dms[h] - mt_w[:, h * LANES:(h + 1) * LANES])
        nums.append(jnp.dot(qk.astype(BF16), v_b[:, h * DVC:(h + 1) * DVC], preferred_element_type=F32))
        den_in = jnp.where(lane == lf0 + h, jnp.sum(qk, axis=-1, keepdims=True), den_in)
    cross_num = jnp.dot(q_b, cbd.astype(BF16), preferred_element_type=F32)
    cross_den = jnp.dot(q_b, nmat.astype(BF16), preferred_element_type=F32)
    den = den_in + w_inter * cross_den
    head_lane = (lane >= lf0) & (lane < lf0 + HC)
    rinv = jnp.where(head_lane, 1.0 / jnp.maximum(jnp.abs(den), jnp.exp(-mt)), 0.0)
    h_out = (jnp.concatenate(nums, axis=1) + _widen(w_inter, to_head) * cross_num) * _widen(rinv, to_head)
    kw = (k * _widen(wend, to_head)).astype(BF16)
    tn = (((0,), (0,)), ((), ()))
    rr = lax.broadcasted_iota(jnp.int32, (WC, WC), 0) // DKC
    cc = lax.broadcasted_iota(jnp.int32, (WC, WC), 1) // DVC
    c_loc = jnp.where(rr == cc, lax.dot_general(kw, v_b, tn, preferred_element_type=F32), 0.0)
    nr = lax.broadcasted_iota(jnp.int32, (WC, LANES), 0) // DKC
    nl = lax.broadcasted_iota(jnp.int32, (WC, LANES), 1)
    n_loc = jnp.where(nl == lf0 + nr,
                      lax.dot_general(kw, jnp.ones((CHUNK, LANES), BF16), tn, preferred_element_type=F32), 0.0)
    cbd_new = cbd * _widen(a_prev, to_head) + c_loc * _widen(a_loc, to_head)
    nmat_new = nmat * a_prev + n_loc * a_loc
    return h_out, cbd_new, nmat_new, m_new


def _mlstm_kernel(qf_ref, kf_ref, vf_ref, gf_ref, qb_ref, kb_ref, vb_ref, gb_ref, gbias_ref,
                  c0_ref, n0_ref, m0_ref, hf_ref, hb_ref, cfin_ref, nfin_ref, mfin_ref,
                  c_scr, n_scr, m_scr, *, nc):
    c = pl.program_id(1)

    @pl.when(c == 0)
    def _():
        c_scr[...] = c0_ref[0]
        n_scr[...] = n0_ref[0]
        m_scr[...] = m0_ref[0]

    hf, cf, nf, mf = _mlstm_dir(qf_ref, kf_ref, vf_ref, gf_ref, gbias_ref, c_scr[0], n_scr[0], m_scr[0], 0)
    hb, cb, nb_, mb = _mlstm_dir(qb_ref, kb_ref, vb_ref, gb_ref, gbias_ref, c_scr[1], n_scr[1], m_scr[1], 1)
    hf_ref[0] = hf
    hb_ref[0] = hb
    c_scr[0] = cf
    c_scr[1] = cb
    n_scr[0] = nf
    n_scr[1] = nb_
    m_scr[0] = mf
    m_scr[1] = mb

    @pl.when(c == nc - 1)
    def _():
        cfin_ref[0] = c_scr[...]
        nfin_ref[0] = n_scr[...]
        mfin_ref[0] = m_scr[...]


def _mlstm(proj, ig_b, fg_b, c0, n0, m0):
    B, L, _ = proj.shape
    nc = L // CHUNK
    gbias = jnp.concatenate([ig_b[0], fg_b[0], ig_b[1], fg_b[1]]).astype(F32)
    gbias = jnp.pad(gbias, (GM_LANE0, LANES - GM_LANE0 - 4 * HC)).reshape(1, LANES)

    def specs(d):
        ch = lambda c: _dir_chunk(c, nc, d)
        col = lambda off: pl.BlockSpec((1, CHUNK, WC), lambda b, c: (b, ch(c), off // WC))
        return [col(C_QM), col(C_KM), col(C_VM),
                pl.BlockSpec((1, CHUNK, LANES), lambda b, c: (b, ch(c), C_DTGM // LANES))]

    cstate = pl.BlockSpec((1, 2, WC, WC), lambda b, c: (b, 0, 0, 0))
    nstate = pl.BlockSpec((1, 2, WC, LANES), lambda b, c: (b, 0, 0, 0))
    mstate = pl.BlockSpec((1, 2, 1, LANES), lambda b, c: (b, 0, 0, 0))
    return pl.pallas_call(
        functools.partial(_mlstm_kernel, nc=nc),
        grid=(B, nc),
        in_specs=specs(0) + specs(1) + [pl.BlockSpec((1, LANES), lambda b, c: (0, 0)), cstate, nstate, mstate],
        out_specs=[pl.BlockSpec((1, CHUNK, WC), lambda b, c: (b, c, 0)),
                   pl.BlockSpec((1, CHUNK, WC), lambda b, c: (b, nc - 1 - c, 0)),
                   cstate, nstate, mstate],
        out_shape=[jax.ShapeDtypeStruct((B, L, WC), F32), jax.ShapeDtypeStruct((B, L, WC), F32),
                   jax.ShapeDtypeStruct((B, 2, WC, WC), F32),
                   jax.ShapeDtypeStruct((B, 2, WC, LANES), F32),
                   jax.ShapeDtypeStruct((B, 2, 1, LANES), F32)],
        scratch_shapes=[pltpu.VMEM((2, WC, WC), F32), pltpu.VMEM((2, WC, LANES), F32),
                        pltpu.VMEM((2, 1, LANES), F32)],
        compiler_params=_cparams("arbitrary", "arbitrary"),
        name="mlstm_scan",
    )(proj, proj, proj, proj, proj, proj, proj, proj, gbias, c0, n0, m0)


def _mlstm_state_in(C, n, m):
    if isinstance(C, int):
        return (jnp.zeros((C, 2, WC, WC), F32), jnp.zeros((C, 2, WC, LANES), F32),
                jnp.zeros((C, 2, 1, LANES), F32))
    B = C.shape[0]
    eye = jnp.eye(HC, dtype=F32)
    c0 = jnp.einsum('bdhkv,hg->bdhkgv', C, eye).reshape(B, 2, WC, WC)
    lanes = jnp.stack([jnp.arange(HC) + _fg_lane(d, 0) for d in range(2)])
    onehot = (lanes[:, :, None] == jnp.arange(LANES)[None, None, :]).astype(F32)
    n0 = jnp.einsum('bdhk,dhl->bdhkl', n, onehot).reshape(B, 2, WC, LANES)
    m0 = jnp.einsum('bdh,dhl->bdl', m, onehot).reshape(B, 2, 1, LANES)
    return c0, n0, m0


def _mlstm_state_out(cfin, nfin, mfin):
    B = cfin.shape[0]
    c5 = cfin.reshape(B, 2, HC, DKC, HC, DVC)
    C = jnp.stack([c5[:, :, h, :, h, :] for h in range(HC)], axis=2)
    n4 = nfin.reshape(B, 2, HC, DKC, LANES)
    n = jnp.stack([jnp.stack([n4[:, d, h, :, _fg_lane(d, h)] for h in range(HC)], axis=1) for d in range(2)], axis=1)
    m = jnp.stack([mfin[:, d, 0, _fg_lane(d, 0):_fg_lane(d, 0) + HC] for d in range(2)], axis=1)
    return C, n, m


RT_IDX0 = 0
RT_GATE0 = TOP_K
RT_ROWS = 2 * TOP_K


def _layer_norm_rows(x, g, b):
    mu = jnp.mean(x, axis=-1, keepdims=True)
    xc = x - mu
    var = jnp.mean(xc * xc, axis=-1, keepdims=True)
    return xc * lax.rsqrt(var + LN_EPS) * g + b


def _outproj_kernel(oa_ref, yf_ref, yb_ref, z_ref, hf_ref, hb_ref, om_ref, x_ref, mod_ref, w_ref,
                    sg_ref, mg_ref, lg_ref, lb_ref, rw_ref, rb_ref, x1_ref, h2_ref, rt_ref):
    z = z_ref[0]
    y = (yf_ref[0] + yb_ref[0]) * (z * _sigmoid(z))
    gw = WB // GB
    ob = []
    for g in range(GB):
        seg = y[:, g * gw:(g + 1) * gw]
        ob.append(seg * lax.rsqrt(jnp.mean(seg * seg, axis=-1, keepdims=True) + LN_EPS))
    ob = jnp.concatenate(ob, axis=1) * sg_ref[...]
    hsum = hf_ref[0] + hb_ref[0]
    oc = []
    for h in range(HC):
        seg = hsum[:, h * DVC:(h + 1) * DVC]
        mu = jnp.mean(seg, axis=-1, keepdims=True)
        sc = seg - mu
        oc.append(sc * lax.rsqrt(jnp.mean(sc * sc, axis=-1, keepdims=True) + LN_EPS))
    oc = jnp.concatenate(oc, axis=1) * mg_ref[...] * _sigmoid(om_ref[0])
    mix_in = jnp.concatenate([oa_ref[0], ob, oc], axis=1).astype(BF16)
    mix = jnp.dot(mix_in, w_ref[...], preferred_element_type=F32)
    x1 = _layer_norm_rows(ALPHA * x_ref[0] + mod_ref[0, 2:3, :] * mix, lg_ref[...], lb_ref[...])
    x1_ref[0] = x1
    h2 = x1 * (1.0 + mod_ref[0, 4:5, :]) + mod_ref[0, 3:4, :]
    for j in range(ROW_CH):
        h2_ref[pl.ds(j, h2.shape[0], stride=ROW_CH), :] = h2[:, j * LANES:(j + 1) * LANES]
    h_hi = h2.astype(BF16)
    h_mid = (h2 - h_hi.astype(F32)).astype(BF16)
    logits = jnp.dot(jnp.concatenate([h_hi, h_hi, h_mid], axis=1), rw_ref[...],
                     preferred_element_type=F32) + rb_ref[...]
    lane = lax.broadcasted_iota(jnp.int32, (1, LANES), 1)
    lane_f = lane.astype(F32)
    lg = jnp.where(lane < N_EXPERTS, logits, -jnp.inf)
    vals, ids = [], []
    for _ in range(TOP_K):
        mx = jnp.max(lg, axis=-1, keepdims=True)
        first = jnp.min(jnp.where(lg == mx, lane_f, float(LANES)), axis=-1, keepdims=True)
        vals.append(mx)
        ids.append(first)
        lg = jnp.where(lane_f == first, -jnp.inf, lg)
    es = [jnp.exp(v - vals[0]) for v in vals]
    inv = 1.0 / (es[0] + es[1] + es[2] + es[3])
    rt = jnp.zeros(logits.shape, F32)
    for k in range(TOP_K):
        rt = jnp.where(lane == RT_IDX0 + k, ids[k], rt)
        rt = jnp.where(lane == RT_GATE0 + k, es[k] * inv, rt)
    rt_ref[...] = rt.T[0:RT_ROWS, :]


def _outproj(oa, yf, yb, hf, hb, proj, x, mod, w_out_b, ssd_g, mlstm_g, ln_g, ln_b, router_w, router_b):
    B, L, _ = x.shape
    tl = min(L, 256)
    bm = mod.shape[0]
    mod_idx = (lambda b, i: (b, 0, 0)) if bm > 1 else (lambda b, i: (0, 0, 0))
    row = lambda w: pl.BlockSpec((1, tl, w), lambda b, i: (b, i, 0))
    const = lambda shape: pl.BlockSpec(shape, lambda b, i: (0,) * len(shape))
    rw = jnp.pad(router_w.astype(F32), ((0, 0), (0, LANES - N_EXPERTS)))
    rw_hi = rw.astype(BF16)
    rw_mid = (rw - rw_hi.astype(F32)).astype(BF16)
    rw = jnp.concatenate([rw_hi, rw_mid, rw_hi], axis=0)
    return pl.pallas_call(
        _outproj_kernel,
        grid=(B, L // tl),
        in_specs=[row(WA), row(WB), row(WB),
                  pl.BlockSpec((1, tl, WB), lambda b, i: (b, i, C_Z // WB)),
                  row(WC), row(WC),
                  pl.BlockSpec((1, tl, WC), lambda b, i: (b, i, C_OM // WC)),
                  row(D_MODEL),
                  pl.BlockSpec((1, 6, D_MODEL), mod_idx),
                  const((D_MIX, D_MODEL)), const((1, WB)), const((1, WC)),
                  const((1, D_MODEL)), const((1, D_MODEL)), const((3 * D_MODEL, LANES)), const((1, LANES))],
        out_specs=[row(D_MODEL),
                   pl.BlockSpec((tl * ROW_CH, LANES), lambda b, i: (b * (L // tl) + i, 0)),
                   pl.BlockSpec((RT_ROWS, tl), lambda b, i: (0, b * (L // tl) + i))],
        out_shape=[jax.ShapeDtypeStruct((B, L, D_MODEL), F32),
                   jax.ShapeDtypeStruct((B * L * ROW_CH, LANES), F32),
                   jax.ShapeDtypeStruct((RT_ROWS, B * L), F32)],
        compiler_params=_cparams("arbitrary", "arbitrary"),
        name="outproj_ln_route",
    )(oa, yf, yb, proj, hf, hb, proj, x, mod, w_out_b, ssd_g.reshape(1, WB), mlstm_g.reshape(1, WC),
      ln_g.reshape(1, D_MODEL), ln_b.reshape(1, D_MODEL), rw, _pad_row(router_b))


def _final_ln_kernel(x_ref, ff_ref, mod_ref, g_ref, b_ref, o_ref):
    tl = x_ref.shape[1]
    ff = jnp.concatenate([ff_ref[pl.ds(j, tl, stride=ROW_CH), :] for j in range(ROW_CH)], axis=1)
    o_ref[0] = _layer_norm_rows(ALPHA * x_ref[0] + mod_ref[0, 5:6, :] * ff, g_ref[...], b_ref[...])


def _final_ln(x1, ff, mod, ln_g, ln_b):
    B, L, _ = x1.shape
    tl = min(L, 512)
    bm = mod.shape[0]
    mod_idx = (lambda b, i: (b, 0, 0)) if bm > 1 else (lambda b, i: (0, 0, 0))
    row = pl.BlockSpec((1, tl, D_MODEL), lambda b, i: (b, i, 0))
    const = pl.BlockSpec((1, D_MODEL), lambda b, i: (0, 0))
    return pl.pallas_call(
        _final_ln_kernel,
        grid=(B, L // tl),
        in_specs=[row, pl.BlockSpec((tl * ROW_CH, LANES), lambda b, i: (b * (L // tl) + i, 0)),
                  pl.BlockSpec((1, 6, D_MODEL), mod_idx), const, const],
        out_specs=row,
        out_shape=jax.ShapeDtypeStruct((B, L, D_MODEL), F32),
        compiler_params=_cparams("arbitrary", "arbitrary"),
        name="final_ln",
    )(x1, ff, mod, ln_g.reshape(1, D_MODEL), ln_b.reshape(1, D_MODEL))


MOE_TM = 2048
MOE_R = 288
MOE_U = SUBLANES
MOE_S = MOE_R + 1
MOE_RB = 512
MOE_PLACE_UNROLL = 4
ROW_CH = D_MODEL // LANES
MOE_LMAX = MOE_TM * TOP_K + N_EXPERTS * MOE_U


def _ceil_to_unit(x):
    return jnp.floor((x + (MOE_U - 1.0)) * (1.0 / MOE_U)) * float(MOE_U)


def _build_routing(rt_ref, d_vm, d_sm, g_sm, idx_sm, gate_sm, sems):
    r = rt_ref[...]
    eio = lax.broadcasted_iota(jnp.int32, (LANES, MOE_TM), 0).astype(F32)
    hit = [r[k:k + 1, :] == eio for k in range(TOP_K)]
    sel = jnp.zeros((LANES, MOE_TM), F32)
    for k in range(TOP_K):
        sel = jnp.where(hit[k], 1.0, sel)
    sel_b = sel.astype(BF16)
    tr = lax.broadcasted_iota(jnp.int32, (MOE_RB, MOE_RB), 0)
    tc = lax.broadcasted_iota(jnp.int32, (MOE_RB, MOE_RB), 1)
    before = (tr < tc).astype(BF16)
    run = jnp.zeros((LANES, 1), F32)
    ranks = []
    for c in range(MOE_TM // MOE_RB):
        blk = slice(c * MOE_RB, (c + 1) * MOE_RB)
        ranks.append(jnp.dot(sel_b[:, blk], before, preferred_element_type=F32) + run)
        run = run + jnp.sum(sel[:, blk], axis=1, keepdims=True)
    rank = jnp.concatenate(ranks, axis=1)
    pc_col = _ceil_to_unit(run)
    er = lax.broadcasted_iota(jnp.int32, (LANES, LANES), 0)
    ec = lax.broadcasted_iota(jnp.int32, (LANES, LANES), 1)
    offs_col = jnp.dot((ec < er).astype(F32), jnp.broadcast_to(pc_col, (LANES, LANES)),
                       precision=HIGHEST, preferred_element_type=F32)[:, 0:1]
    cnt_row = lax.dot_general(jnp.ones((SUBLANES, MOE_TM), BF16), sel_b, (((1,), (1,)), ((), ())),
                              preferred_element_type=F32)
    offs_row = jnp.dot(_ceil_to_unit(cnt_row), (er < ec).astype(F32), precision=HIGHEST,
                       preferred_element_type=F32)
    dest = offs_col + rank
    dest4 = [jnp.sum(jnp.where(hit[k], dest, 0.0), axis=0, keepdims=True) for k in range(TOP_K)]
    tail = jnp.concatenate([offs_row[0:RT_ROWS - TOP_K, :],
                            jnp.zeros((RT_ROWS - TOP_K, MOE_TM - LANES), F32)], axis=1)
    d_vm[...] = jnp.concatenate(dest4 + [tail], axis=0).astype(jnp.int32)
    cp_d = pltpu.make_async_copy(d_vm, d_sm, sems.at[0])
    cp_g = pltpu.make_async_copy(rt_ref, g_sm, sems.at[1])
    cp_d.start()
    cp_g.start()
    cp_d.wait()
    cp_g.wait()

    def pad_body(ex, c):
        end = d_sm[TOP_K, ex + 1]
        for u in range(MOE_U):
            pos = jnp.maximum(end - 1 - u, 0)
            idx_sm[pos] = MOE_TM
            gate_sm[pos] = 0.0
        return c

    lax.fori_loop(0, N_EXPERTS, pad_body, 0)

    def place_body(tb, c):
        for u in range(MOE_PLACE_UNROLL):
            t = tb * MOE_PLACE_UNROLL + u
            for k in range(TOP_K):
                pos = d_sm[k, t]
                idx_sm[pos] = t
                gate_sm[pos] = g_sm[RT_GATE0 + k, t]
        return c

    lax.fori_loop(0, MOE_TM // MOE_PLACE_UNROLL, place_body, 0)


def _moe_kernel(rt_ref, x_ref, wgu_ref, bgu_ref, wdn_ref, bdn_ref, o_hbm,
                acc, xt, yt, d_vm, d_sm, g_sm, idx_sm, gate_sm, sems):
    i = pl.program_id(0)
    e = pl.program_id(1)

    @pl.when((i == 0) & (e == 0))
    def _():
        xt[...] = jnp.zeros(xt.shape, F32)

    @pl.when(e == 0)
    def _():
        acc[...] = jnp.zeros(acc.shape, F32)
        _build_routing(rt_ref, d_vm, d_sm, g_sm, idx_sm, gate_sm, sems)

    base = d_sm[TOP_K, e]
    cnt = d_sm[TOP_K, e + 1] - base

    def chunk_body(cix, carry):
        r0 = base + cix * MOE_R
        ngrp = lax.shift_right_logical(jnp.minimum(MOE_R, cnt - cix * MOE_R), 3)

        def gather(gi, c2):
            for u in range(MOE_U):
                mi = gi * MOE_U + u
                t = jnp.minimum(idx_sm[r0 + mi], MOE_TM - 1)
                src = pl.multiple_of(t * ROW_CH, ROW_CH)
                xt[pl.ds(mi, ROW_CH, stride=MOE_S), :] = x_ref[pl.ds(src, ROW_CH), :]
            return c2

        lax.fori_loop(0, ngrp, gather, 0)
        x = jnp.concatenate([xt[pl.ds(j * MOE_S, MOE_R), :] for j in range(ROW_CH)], axis=1).astype(BF16)
        gu = jnp.dot(x, wgu_ref[0], preferred_element_type=F32) + bgu_ref[0]
        g = jnp.minimum(gu[:, :D_FF], SWIGLU_LIMIT)
        u_ = jnp.clip(gu[:, D_FF:], -SWIGLU_LIMIT, SWIGLU_LIMIT)
        act = g * _sigmoid(SWIGLU_ALPHA * g) * (u_ + 1.0)
        y = jnp.dot(act.astype(BF16), wdn_ref[0], preferred_element_type=F32) + bdn_ref[0]
        for j in range(ROW_CH):
            yt[pl.ds(j * MOE_S, MOE_R), :] = y[:, j * LANES:(j + 1) * LANES]

        def scatter(gi, c2):
            upd = []
            for u in range(MOE_U):
                mi = gi * MOE_U + u
                dst = pl.multiple_of(idx_sm[r0 + mi] * ROW_CH, ROW_CH)
                gt = gate_sm[r0 + mi]
                upd.append((dst, acc[pl.ds(dst, ROW_CH), :] + gt * yt[pl.ds(mi, ROW_CH, stride=MOE_S), :]))
            for dst, val in upd:
                acc[pl.ds(dst, ROW_CH), :] = val
            return c2

        lax.fori_loop(0, ngrp, scatter, 0)
        return carry

    lax.fori_loop(0, lax.div(cnt + (MOE_R - 1), MOE_R), chunk_body, 0)

    @pl.when(e == N_EXPERTS - 1)
    def _():
        rows = MOE_TM * ROW_CH
        cp = pltpu.make_async_copy(acc.at[pl.ds(0, rows), :],
                                   o_hbm.at[pl.ds(pl.multiple_of(i * rows, rows), rows), :], sems.at[2])
        cp.start()
        cp.wait()


def _moe(xv, rt, w_gu_b, b_gu, w_dn_b, b_dn):
    T = xv.shape[0] // ROW_CH
    nt = T // MOE_TM
    return pl.pallas_call(
        _moe_kernel,
        grid=(nt, N_EXPERTS),
        in_specs=[pl.BlockSpec((RT_ROWS, MOE_TM), lambda i, e: (0, i)),
                  pl.BlockSpec((MOE_TM * ROW_CH, LANES), lambda i, e: (i, 0)),
                  pl.BlockSpec((1, D_MODEL, 2 * D_FF), lambda i, e: (e, 0, 0)),
                  pl.BlockSpec((1, 1, 2 * D_FF), lambda i, e: (e, 0, 0)),
                  pl.BlockSpec((1, D_FF, D_MODEL), lambda i, e: (e, 0, 0)),
                  pl.BlockSpec((1, 1, D_MODEL), lambda i, e: (e, 0, 0))],
        out_specs=pl.BlockSpec(memory_space=pl.ANY),
        scratch_shapes=[pltpu.VMEM(((MOE_TM + 1) * ROW_CH, LANES), F32),
                        pltpu.VMEM((ROW_CH * MOE_S, LANES), F32),
                        pltpu.VMEM((ROW_CH * MOE_S, LANES), F32),
                        pltpu.VMEM((RT_ROWS, MOE_TM), jnp.int32),
                        pltpu.SMEM((RT_ROWS, MOE_TM), jnp.int32),
                        pltpu.SMEM((RT_ROWS, MOE_TM), F32),
                        pltpu.SMEM((MOE_LMAX,), jnp.int32),
                        pltpu.SMEM((MOE_LMAX,), F32),
                        pltpu.SemaphoreType.DMA((3,))],
        out_shape=jax.ShapeDtypeStruct((T * ROW_CH, LANES), F32),
        compiler_params=_cparams("arbitrary", "arbitrary"),
        name="moe_ffn",
    )(rt, xv, w_gu_b, b_gu.reshape(N_EXPERTS, 1, 2 * D_FF), w_dn_b, b_dn.reshape(N_EXPERTS, 1, D_MODEL))


def _trunk_layer(x, mod, p, layer, ctx):
    B, L, _ = x.shape
    proj = _inproj(x, mod, p['w_in'], rope=ctx is not None)
    kt, v = _attn_operands(proj, None if ctx is None else ctx['k'], None if ctx is None else ctx['v'])
    oa = _attention(proj, kt, v, p['lam'], p['attn_g'], layer)
    if ctx is None:
        h0 = jnp.zeros((B, 2, NB, WB), F32)
        c0, n0, m0 = _mlstm_state_in(B, None, None)
    else:
        h0 = _ssd_state_in(ctx['ssd'])
        c0, n0, m0 = _mlstm_state_in(ctx['C'], ctx['n'], ctx['m'])
    yf, yb, hfin = _ssd(proj, p['conv_w'], p['conv_b'], p['dt_bias'], p['a_log'], p['d_skip'], h0)
    hf, hb, cfin, nfin, mfin = _mlstm(proj, p['ig_b'], p['fg_b'], c0, n0, m0)
    x1, h2, rt = _outproj(oa, yf, yb, hf, hb, proj, x, mod, p['w_out'], p['ssd_g'], p['mlstm_g'],
                          p['ln1_g'], p['ln1_b'], p['router_w'], p['router_b'])
    ff = _moe(h2, rt, p['w_gu'], p['b_gu'], p['w_dn'], p['b_dn'])
    x2 = _final_ln(x1, ff, mod, p['ln2_g'], p['ln2_b'])
    new_ctx = None
    if ctx is None:
        c_out, n_out, m_out = _mlstm_state_out(cfin, nfin, mfin)
        new_ctx = (proj[:, :, C_KA:C_KA + WA].reshape(B, L, HA, 2 * DQK),
                   proj[:, :, C_VA:C_VA + WA].reshape(B, L, HA, DVA),
                   _ssd_state_out(hfin), c_out, n_out, m_out)
    return x2, new_ctx


def kernel(x_prompt, x_sample, c, cache_attn_k, cache_attn_v, state_ssd, state_mlstm_C, state_mlstm_n, state_mlstm_m, c_ctx, w_mod, b_mod, w_in, lam_q1, lam_k1, lam_q2, lam_k2, attn_g, conv_w, conv_b, dt_bias, a_log, d_skip, ssd_g, ig_b, fg_b, mlstm_g, w_out, ln1_g, ln1_b, router_w, router_b, w_gu, b_gu, w_dn, b_dn, ln2_g, ln2_b):
    nb = c.shape[0]
    cond = jnp.concatenate([c_ctx[None, :], c, jnp.zeros((2 * SUBLANES - 1 - nb, D_MODEL), F32)], axis=0)
    y_prompt, y_sample = x_prompt, x_sample
    outs = [[] for _ in range(6)]
    for l in range(DEPTH):
        p = {'w_in': _permute_w_in(w_in[l]),
             'lam': jnp.stack([lam_q1[l], lam_k1[l], lam_q2[l], lam_k2[l]]),
             'attn_g': attn_g[l], 'conv_w': conv_w[l], 'conv_b': conv_b[l], 'dt_bias': dt_bias[l],
             'a_log': a_log[l], 'd_skip': d_skip[l], 'ssd_g': ssd_g[l], 'ig_b': ig_b[l], 'fg_b': fg_b[l],
             'mlstm_g': mlstm_g[l], 'w_out': w_out[l].astype(BF16), 'ln1_g': ln1_g[l], 'ln1_b': ln1_b[l],
             'router_w': router_w[l], 'router_b': router_b[l], 'w_gu': w_gu[l].astype(BF16), 'b_gu': b_gu[l],
             'w_dn': w_dn[l].astype(BF16), 'b_dn': b_dn[l], 'ln2_g': ln2_g[l], 'ln2_b': ln2_b[l]}
        mod = _modulation(cond, w_mod[l], b_mod[l]).reshape(2 * SUBLANES, 6, D_MODEL)
        y_prompt, st = _trunk_layer(y_prompt, mod[0:1], p, l, None)
        for acc_list, s in zip(outs, st):
            acc_list.append(s)
        ctx = {'k': cache_attn_k[:, l], 'v': cache_attn_v[:, l], 'ssd': state_ssd[:, l],
               'C': state_mlstm_C[:, l], 'n': state_mlstm_n[:, l], 'm': state_mlstm_m[:, l]}
        y_sample, _ = _trunk_layer(y_sample, mod[1:1 + nb], p, l, ctx)
    return (y_prompt, y_sample) + tuple(jnp.stack(o, axis=1) for o in outs)
```

```python
import functools
import math

import jax
import jax.numpy as jnp
import numpy as np
from jax import lax
from jax.experimental import pallas as pl
from jax.experimental.pallas import tpu as pltpu

F32 = jnp.float32
BF16 = jnp.bfloat16
HIGHEST = lax.Precision.HIGHEST

D_MODEL = 1024
DEPTH = 2
GRID_W = 64
HA = 4
DQK = 32
DVA = 2 * DQK
WA = HA * DVA
ROPE_BASE = 10000.0
HB = 8
PB = 64
WB = HB * PB
GB = 2
NB = 64
CONV_W = 3
CONV_CH = WB + 2 * GB * NB
HC = 4
DKC = 64
DVC = 64
WC = HC * DVC
D_MIX = WA + WB + WC
CHUNK = 128
N_EXPERTS = 32
TOP_K = 4
D_FF = D_MODEL
SWIGLU_LIMIT = 7.0
SWIGLU_ALPHA = 1.702
ALPHA = (2 * DEPTH) ** 0.25
LN_EPS = 1e-5

C_QA, C_KA, C_VA, C_OM = 0, 256, 512, 768
C_Z = 1024
C_XBC = 1536
C_QM, C_KM, C_VM = 2304, 2560, 2816
C_DTGM = 3072
N_PROJ = 3200
DT_LANES = 2 * HB
GM_LANE0 = DT_LANES

VMEM_LIMIT = 56 * 1024 * 1024
LANES = 128
SUBLANES = 8


def _cparams(*sem):
    return pltpu.CompilerParams(dimension_semantics=sem, vmem_limit_bytes=VMEM_LIMIT)


def _sigmoid(x):
    return 1.0 / (1.0 + jnp.exp(-x))


def _softplus(x):
    return jnp.maximum(x, 0.0) + jnp.log1p(jnp.exp(-jnp.abs(x)))


def _mod_kernel(c_ref, w_ref, b_ref, o_ref):
    c = c_ref[...]
    s = (c * _sigmoid(c)).astype(BF16)
    o_ref[...] = jnp.dot(s, w_ref[...].astype(BF16), preferred_element_type=F32) + b_ref[...]


def _modulation(cond, w_mod, b_mod):
    rows = cond.shape[0]
    n = w_mod.shape[1]
    tn = D_MODEL
    return pl.pallas_call(
        _mod_kernel,
        grid=(n // tn,),
        in_specs=[pl.BlockSpec((rows, D_MODEL), lambda j: (0, 0)),
                  pl.BlockSpec((D_MODEL, tn), lambda j: (0, j)),
                  pl.BlockSpec((1, tn), lambda j: (0, j))],
        out_specs=pl.BlockSpec((rows, tn), lambda j: (0, j)),
        out_shape=jax.ShapeDtypeStruct((rows, n), F32),
        compiler_params=_cparams("arbitrary"),
        name="modulation",
    )(cond, w_mod, b_mod.reshape(1, n))


def _inproj_kernel(x_ref, mod_ref, w_ref, cos_ref, sa_ref, sb_ref, o_ref, *, rope):
    x = x_ref[0]
    sh = mod_ref[0, 0:1, :]
    sc = mod_ref[0, 1:2, :]
    h = (x * (1.0 + sc) + sh).astype(BF16)
    p = jnp.dot(h, w_ref[...], preferred_element_type=F32)

    def rot(t):
        return (t * cos_ref[...] + pltpu.roll(t, WA - DQK // 4, 1) * sa_ref[...]
                + pltpu.roll(t, DQK // 4, 1) * sb_ref[...])

    q = p[:, C_QA:C_QA + WA]
    k = p[:, C_KA:C_KA + WA]
    if rope:
        q = rot(q)
        k = rot(k)
    o_ref[0, :, C_QA:C_QA + WA] = q * (DQK ** -0.5)
    o_ref[0, :, C_KA:C_KA + WA] = k
    o_ref[0, :, C_VA:] = p[:, C_VA:]


def _rope_tables(L):
    quarter = DQK // 4
    pos = np.arange(L)
    row = pos // GRID_W
    col = pos % GRID_W
    inv = ROPE_BASE ** (-np.arange(quarter, dtype=np.float32) / quarter)
    lane = np.arange(WA)
    c = lane % DQK
    use_col = (c // (DQK // 2)) == 1
    w = c % (DQK // 2)
    f = w % quarter
    first = w < quarter
    p = jnp.where(use_col[None, :], col[:, None], row[:, None]).astype(F32)
    ang = p * jnp.asarray(inv)[f][None, :]
    cos = jnp.cos(ang)
    sin = jnp.sin(ang)
    sa = jnp.where(first[None, :], -sin, 0.0)
    sb = jnp.where(first[None, :], 0.0, sin)
    return cos, sa, sb


def _inproj(x, mod, w_in_p, rope):
    B, L, _ = x.shape
    tl = min(L, 512)
    cos, sa, sb = _rope_tables(L)
    bm = mod.shape[0]
    mod_idx = (lambda b, i: (b, 0, 0)) if bm > 1 else (lambda b, i: (0, 0, 0))
    tab = pl.BlockSpec((tl, WA), lambda b, i: (i, 0))
    return pl.pallas_call(
        functools.partial(_inproj_kernel, rope=rope),
        grid=(B, L // tl),
        in_specs=[pl.BlockSpec((1, tl, D_MODEL), lambda b, i: (b, i, 0)),
                  pl.BlockSpec((1, 6, D_MODEL), mod_idx),
                  pl.BlockSpec((D_MODEL, N_PROJ), lambda b, i: (0, 0)),
                  tab, tab, tab],
        out_specs=pl.BlockSpec((1, tl, N_PROJ), lambda b, i: (b, i, 0)),
        out_shape=jax.ShapeDtypeStruct((B, L, N_PROJ), F32),
        compiler_params=_cparams("arbitrary", "arbitrary"),
        name="inproj",
    )(x, mod, w_in_p, cos, sa, sb)


def _permute_w_in(w_in):
    sizes = (WA, WA, WA, WB, CONV_CH, 2 * HB, WC, WC, WC, WC, 4 * HC)
    offs = np.concatenate([[0], np.cumsum(sizes)])
    qa, ka, va, z, xbc, dt, qm, km, vm, om, gm = (w_in[:, offs[i]:offs[i + 1]] for i in range(11))
    pad = jnp.zeros((w_in.shape[0], N_PROJ - C_DTGM - 2 * HB - 4 * HC), w_in.dtype)
    return jnp.concatenate([qa, ka, va, om, z, xbc, qm, km, vm, dt, gm, pad], axis=1).astype(BF16)


def _attn_kernel(lamp_ref, q_ref, kt_ref, v_ref, g_ref, o_ref, *, lam_init):
    lp = lamp_ref[...]
    lam = (jnp.exp(jnp.sum(lp[0:1] * lp[1:2], axis=-1, keepdims=True))
           - jnp.exp(jnp.sum(lp[2:3] * lp[3:4], axis=-1, keepdims=True)) + lam_init)
    q = q_ref[0]
    outs = []
    for h in range(HA):
        os_ = []
        for m in range(2):
            c0 = (2 * h + m) * DQK
            s = jnp.dot(q[:, c0:c0 + DQK].astype(BF16), kt_ref[0, c0:c0 + DQK, :],
                        preferred_element_type=F32)
            p = jnp.exp(s - jnp.max(s, axis=-1, keepdims=True)).astype(BF16)
            os_.append(jnp.dot(p, v_ref[0, h], preferred_element_type=F32))
        seg = (os_[0][:, :DVA] * (1.0 / os_[0][:, DVA:DVA + 1])
               - os_[1][:, :DVA] * (lam / os_[1][:, DVA:DVA + 1]))
        ms = jnp.mean(seg * seg, axis=-1, keepdims=True)
        outs.append(seg * lax.rsqrt(ms + LN_EPS))
    o_ref[0] = jnp.concatenate(outs, axis=1) * g_ref[...] * (1.0 - lam_init)


def _attention(proj, kt, v, lam_params, attn_g, layer):
    B, L, _ = proj.shape
    lk = kt.shape[2]
    tq = min(L, 256)
    lam_init = 0.8 - 0.6 * math.exp(-0.3 * layer)
    return pl.pallas_call(
        functools.partial(_attn_kernel, lam_init=lam_init),
        grid=(B, L // tq),
        in_specs=[pl.BlockSpec((4, DQK), lambda b, i: (0, 0)),
                  pl.BlockSpec((1, tq, WA), lambda b, i: (b, i, C_QA // WA)),
                  pl.BlockSpec((1, WA, lk), lambda b, i: (b, 0, 0)),
                  pl.BlockSpec((1, HA, lk, LANES), lambda b, i: (b, 0, 0, 0)),
                  pl.BlockSpec((1, WA), lambda b, i: (0, 0))],
        out_specs=pl.BlockSpec((1, tq, WA), lambda b, i: (b, i, 0)),
        out_shape=jax.ShapeDtypeStruct((B, L, WA), F32),
        compiler_params=_cparams("arbitrary", "arbitrary"),
        name="diff_attention",
    )(lam_params, proj, kt, v, jnp.tile(attn_g, HA).reshape(1, WA))


def _attn_operands(proj, ctx_k, ctx_v):
    B, L, _ = proj.shape
    k = proj[:, :, C_KA:C_KA + WA]
    v = proj[:, :, C_VA:C_VA + WA]
    if ctx_k is not None:
        k = jnp.concatenate([k, ctx_k.reshape(B, -1, WA)], axis=1)
        v = jnp.concatenate([v, ctx_v.reshape(B, -1, WA)], axis=1)
    lk = v.shape[1]
    vh = jnp.swapaxes(v.reshape(B, lk, HA, DVA), 1, 2).astype(BF16)
    vaug = jnp.concatenate([vh, jnp.ones((B, HA, lk, 1), BF16),
                            jnp.zeros((B, HA, lk, LANES - DVA - 1), BF16)], axis=-1)
    return jnp.swapaxes(k, 1, 2).astype(BF16), vaug


def _tri(rev):
    i = lax.broadcasted_iota(jnp.int32, (CHUNK, CHUNK), 0)
    j = lax.broadcasted_iota(jnp.int32, (CHUNK, CHUNK), 1)
    return (j >= i) if rev else (j <= i)


def _split3(x):
    hi = x.astype(BF16)
    r1 = x - hi.astype(F32)
    mid = r1.astype(BF16)
    lo = (r1 - mid.astype(F32)).astype(BF16)
    return hi, mid, lo


def _chunk_cumsum(a, rev):
    tri = _tri(rev).astype(BF16)
    return jnp.dot(jnp.concatenate([tri, tri, tri], axis=1), jnp.concatenate(_split3(a), axis=0),
                   preferred_element_type=F32)


def _widen(t, onehot3):
    return jnp.dot(jnp.concatenate(_split3(t), axis=1), onehot3, preferred_element_type=F32)


def _onehot3(src_lane_of_col, width):
    r = lax.broadcasted_iota(jnp.int32, (3 * LANES, width), 0) % LANES
    c = lax.broadcasted_iota(jnp.int32, (3 * LANES, width), 1)
    return (r == src_lane_of_col(c)).astype(BF16)


def _dir_chunk(c, nc, d):
    return c if d == 0 else nc - 1 - c


def _ssd_dir(x_ref, xp_ref, xn_ref, dt_ref, cw_ref, cb_ref, dtb_ref, alog_ref, ht, chunk, nc, d):
    rev = d == 1
    x = x_ref[0]
    rowid = lax.broadcasted_iota(jnp.int32, (CHUNK, 1), 0)
    prev = jnp.where(chunk == 0, 0.0, xp_ref[0, SUBLANES - 1:SUBLANES, :])
    nxt = jnp.where(chunk == nc - 1, 0.0, xn_ref[0, 0:1, :])
    xm1 = jnp.where(rowid == 0, prev, pltpu.roll(x, 1, 0))
    xp1 = jnp.where(rowid == CHUNK - 1, nxt, pltpu.roll(x, CHUNK - 1, 0))
    xc = cw_ref[0:1, :] * xm1 + cw_ref[1:2, :] * x + cw_ref[2:3, :] * xp1 + cb_ref[...]
    xc = xc * _sigmoid(xc)
    xs = xc[:, :WB]

    lane = lax.broadcasted_iota(jnp.int32, (1, LANES), 1)
    dt = _softplus(dt_ref[0] + dtb_ref[...])
    aneg = jnp.where(lane < DT_LANES, -jnp.exp(alog_ref[...]), 0.0)
    cum = _chunk_cumsum(dt * aneg, rev)
    cum_t = cum.T
    expand = _onehot3(lambda c: d * HB + c // PB, WB)
    dt_w = _widen(dt, expand)
    cum_w = _widen(cum, expand)
    tot_w = cum_w[0:1, :] if rev else cum_w[CHUNK - 1:CHUNK, :]
    e_w = jnp.exp(cum_w)
    te_w = jnp.exp(tot_w - cum_w)
    xin = xs * dt_w
    xw = (xin * te_w).astype(BF16)
    xin_b = xin.astype(BF16)
    mask = _tri(rev)
    ht_b = ht.astype(BF16)
    ys, sts = [], []
    gw = WB // GB
    for g in range(GB):
        bm = xc[:, WB + g * NB:WB + (g + 1) * NB].astype(BF16)
        cm = xc[:, WB + GB * NB + g * NB:WB + GB * NB + (g + 1) * NB].astype(BF16)
        cb = lax.dot_general(cm, bm, (((1,), (1,)), ((), ())), preferred_element_type=F32)
        yd = []
        for r in range(HB // GB):
            h = g * (HB // GB) + r
            hd = d * HB + h
            seg = cum[:, hd:hd + 1] - cum_t[hd:hd + 1, :]
            dec = jnp.exp(jnp.where(mask, seg, -jnp.inf))
            yd.append(jnp.dot((cb * dec).astype(BF16), xin_b[:, h * PB:(h + 1) * PB],
                              preferred_element_type=F32))
        y_off = jnp.dot(cm, ht_b[:, g * gw:(g + 1) * gw], preferred_element_type=F32)
        ys.append(jnp.concatenate(yd, axis=1) + y_off * e_w[:, g * gw:(g + 1) * gw])
        sts.append(lax.dot_general(bm, xw[:, g * gw:(g + 1) * gw], (((0,), (0,)), ((), ())),
                                   preferred_element_type=F32))
    y = jnp.concatenate(ys, axis=1)
    cd = e_w[0:1, :] if rev else e_w[CHUNK - 1:CHUNK, :]
    ht_new = ht * cd + jnp.concatenate(sts, axis=1)
    return y, xs, ht_new


def _ssd_kernel(xf_ref, xfp_ref, xfn_ref, dtf_ref, xb_ref, xbp_ref, xbn_ref, dtb_ref,
                cw_ref, cb_ref, dtbias_ref, alog_ref, dskip_ref, h0_ref,
                yf_ref, yb_ref, hfin_ref, h_scr, *, nc):
    c = pl.program_id(1)

    @pl.when(c == 0)
    def _():
        h_scr[...] = h0_ref[0]

    yf, xs_f, hf = _ssd_dir(xf_ref, xfp_ref, xfn_ref, dtf_ref, cw_ref, cb_ref, dtbias_ref, alog_ref,
                            h_scr[0], c, nc, 0)
    yb, _, hb = _ssd_dir(xb_ref, xbp_ref, xbn_ref, dtb_ref, cw_ref, cb_ref, dtbias_ref, alog_ref,
                         h_scr[1], nc - 1 - c, nc, 1)
    yf_ref[0] = yf + xs_f * dskip_ref[...]
    yb_ref[0] = yb
    h_scr[0] = hf
    h_scr[1] = hb

    @pl.when(c == nc - 1)
    def _():
        hfin_ref[0] = h_scr[...]


def _pad_row(v, width=LANES):
    v = v.reshape(1, -1).astype(F32)
    return jnp.pad(v, ((0, 0), (0, width - v.shape[1])))


def _ssd(proj, conv_w, conv_b, dt_bias, a_log, d_skip, h0):
    B, L, _ = proj.shape
    nc = L // CHUNK
    r8 = CHUNK // SUBLANES
    nb8 = L // SUBLANES
    xblk = C_XBC // CONV_CH
    dblk = C_DTGM // LANES

    def specs(d):
        ch = lambda c: _dir_chunk(c, nc, d)
        return [
            pl.BlockSpec((1, CHUNK, CONV_CH), lambda b, c: (b, ch(c), xblk)),
            pl.BlockSpec((1, SUBLANES, CONV_CH), lambda b, c: (b, jnp.maximum(ch(c) * r8 - 1, 0), xblk)),
            pl.BlockSpec((1, SUBLANES, CONV_CH), lambda b, c: (b, jnp.minimum((ch(c) + 1) * r8, nb8 - 1), xblk)),
            pl.BlockSpec((1, CHUNK, LANES), lambda b, c: (b, ch(c), dblk)),
        ]

    const = lambda shape: pl.BlockSpec(shape, lambda b, c: (0,) * len(shape))
    state = pl.BlockSpec((1, 2, NB, WB), lambda b, c: (b, 0, 0, 0))
    return pl.pallas_call(
        functools.partial(_ssd_kernel, nc=nc),
        grid=(B, nc),
        in_specs=specs(0) + specs(1) + [const((CONV_W, CONV_CH)), const((1, CONV_CH)), const((1, LANES)),
                                        const((1, LANES)), const((1, WB)), state],
        out_specs=[pl.BlockSpec((1, CHUNK, WB), lambda b, c: (b, c, 0)),
                   pl.BlockSpec((1, CHUNK, WB), lambda b, c: (b, nc - 1 - c, 0)),
                   state],
        out_shape=[jax.ShapeDtypeStruct((B, L, WB), F32), jax.ShapeDtypeStruct((B, L, WB), F32),
                   jax.ShapeDtypeStruct((B, 2, NB, WB), F32)],
        scratch_shapes=[pltpu.VMEM((2, NB, WB), F32)],
        compiler_params=_cparams("arbitrary", "arbitrary"),
        name="ssd_scan",
    )(proj, proj, proj, proj, proj, proj, proj, proj,
      conv_w, conv_b.reshape(1, CONV_CH), _pad_row(dt_bias), _pad_row(a_log),
      jnp.repeat(d_skip, PB).reshape(1, WB), h0)


def _ssd_state_in(state):
    B = state.shape[0]
    return jnp.transpose(state, (0, 1, 4, 2, 3)).reshape(B, 2, NB, WB)


def _ssd_state_out(ht):
    B = ht.shape[0]
    return jnp.transpose(ht.reshape(B, 2, NB, HB, PB), (0, 1, 3, 4, 2))


def _fg_lane(d, h):
    return GM_LANE0 + d * 2 * HC + HC + h


def _mlstm_dir(q_ref, k_ref, v_ref, g_ref, gbias_ref, cbd, nmat, m_prev, d):
    rev = d == 1
    q_b = q_ref[0].astype(BF16)
    k = k_ref[0] * (DKC ** -0.5)
    k_b = k.astype(BF16)
    v_b = v_ref[0].astype(BF16)
    pre = g_ref[0] + gbias_ref[...]
    bc = _chunk_cumsum(-_softplus(-pre), rev)
    ig = pltpu.roll(pre, HC, 1)
    tot = bc[0:1, :] if rev else bc[CHUNK - 1:CHUNK, :]
    gend = tot - bc + ig
    mloc = jnp.max(gend, axis=0, keepdims=True)
    wend = jnp.exp(gend - mloc)
    m_new = jnp.maximum(tot + m_prev, mloc)
    a_prev = jnp.exp(tot + m_prev - m_new)
    a_loc = jnp.exp(mloc - m_new)
    inter = bc + m_prev
    bc_t = bc.T
    ig_t = ig.T
    lf0 = _fg_lane(d, 0)
    to_tile = _onehot3(lambda c: lf0 + c // LANES, HC * LANES)
    to_head = _onehot3(lambda c: lf0 + c // DVC, WC)
    lane = lax.broadcasted_iota(jnp.int32, (1, LANES), 1)
    mask = _tri(rev)
    bc_w = _widen(bc, to_tile)
    dms = []
    rmax = jnp.zeros((CHUNK, LANES), F32)
    for h in range(HC):
        lf = lf0 + h
        dm = jnp.where(mask, bc_w[:, h * LANES:(h + 1) * LANES] - bc_t[lf:lf + 1, :] + ig_t[lf:lf + 1, :],
                       -jnp.inf)
        dms.append(dm)
        rmax = jnp.where(lane == lf, jnp.max(dm, axis=-1, keepdims=True), rmax)
    mt = jnp.maximum(inter, rmax)
    mt_w = _widen(mt, to_tile)
    w_inter = jnp.exp(inter - mt)
    nums = []
    den_in = jnp.zeros((CHUNK, LANES), F32)
    for h in range(HC):
        hk = slice(h * DKC, (h + 1) * DKC)
        qk = lax.dot_general(q_b[:, hk], k_b[:, hk], (((1,), (1,)), ((), ())), preferred_element_type=F32)
        qk = qk * jnp.exp(dms[h] - mt_w[:, h * LANES:(h + 1) * LANES])
        nums.append(jnp.dot(qk.astype(BF16), v_b[:, h * DVC:(h + 1) * DVC], preferred_element_type=F32))
        den_in = jnp.where(lane == lf0 + h, jnp.sum(qk, axis=-1, keepdims=True), den_in)
    cross_num = jnp.dot(q_b, cbd.astype(BF16), preferred_element_type=F32)
    cross_den = jnp.dot(q_b, nmat.astype(BF16), preferred_element_type=F32)
    den = den_in + w_inter * cross_den
    head_lane = (lane >= lf0) & (lane < lf0 + HC)
    rinv = jnp.where(head_lane, 1.0 / jnp.maximum(jnp.abs(den), jnp.exp(-mt)), 0.0)
    h_out = (jnp.concatenate(nums, axis=1) + _widen(w_inter, to_head) * cross_num) * _widen(rinv, to_head)
    kw = (k * _widen(wend, to_head)).astype(BF16)
    tn = (((0,), (0,)), ((), ()))
    rr = lax.broadcasted_iota(jnp.int32, (WC, WC), 0) // DKC
    cc = lax.broadcasted_iota(jnp.int32, (WC, WC), 1) // DVC
    c_loc = jnp.where(rr == cc, lax.dot_general(kw, v_b, tn, preferred_element_type=F32), 0.0)
    nr = lax.broadcasted_iota(jnp.int32, (WC, LANES), 0) // DKC
    nl = lax.broadcasted_iota(jnp.int32, (WC, LANES), 1)
    n_loc = jnp.where(nl == lf0 + nr,
                      lax.dot_general(kw, jnp.ones((CHUNK, LANES), BF16), tn, preferred_element_type=F32), 0.0)
    cbd_new = cbd * _widen(a_prev, to_head) + c_loc * _widen(a_loc, to_head)
    nmat_new = nmat * a_prev + n_loc * a_loc
    return h_out, cbd_new, nmat_new, m_new


def _mlstm_kernel(qf_ref, kf_ref, vf_ref, gf_ref, qb_ref, kb_ref, vb_ref, gb_ref, gbias_ref,
                  c0_ref, n0_ref, m0_ref, hf_ref, hb_ref, cfin_ref, nfin_ref, mfin_ref,
                  c_scr, n_scr, m_scr, *, nc):
    c = pl.program_id(1)

    @pl.when(c == 0)
    def _():
        c_scr[...] = c0_ref[0]
        n_scr[...] = n0_ref[0]
        m_scr[...] = m0_ref[0]

    hf, cf, nf, mf = _mlstm_dir(qf_ref, kf_ref, vf_ref, gf_ref, gbias_ref, c_scr[0], n_scr[0], m_scr[0], 0)
    hb, cb, nb_, mb = _mlstm_dir(qb_ref, kb_ref, vb_ref, gb_ref, gbias_ref, c_scr[1], n_scr[1], m_scr[1], 1)
    hf_ref[0] = hf
    hb_ref[0] = hb
    c_scr[0] = cf
    c_scr[1] = cb
    n_scr[0] = nf
    n_scr[1] = nb_
    m_scr[0] = mf
    m_scr[1] = mb

    @pl.when(c == nc - 1)
    def _():
        cfin_ref[0] = c_scr[...]
        nfin_ref[0] = n_scr[...]
        mfin_ref[0] = m_scr[...]


def _mlstm(proj, ig_b, fg_b, c0, n0, m0):
    B, L, _ = proj.shape
    nc = L // CHUNK
    gbias = jnp.concatenate([ig_b[0], fg_b[0], ig_b[1], fg_b[1]]).astype(F32)
    gbias = jnp.pad(gbias, (GM_LANE0, LANES - GM_LANE0 - 4 * HC)).reshape(1, LANES)

    def specs(d):
        ch = lambda c: _dir_chunk(c, nc, d)
        col = lambda off: pl.BlockSpec((1, CHUNK, WC), lambda b, c: (b, ch(c), off // WC))
        return [col(C_QM), col(C_KM), col(C_VM),
                pl.BlockSpec((1, CHUNK, LANES), lambda b, c: (b, ch(c), C_DTGM // LANES))]

    cstate = pl.BlockSpec((1, 2, WC, WC), lambda b, c: (b, 0, 0, 0))
    nstate = pl.BlockSpec((1, 2, WC, LANES), lambda b, c: (b, 0, 0, 0))
    mstate = pl.BlockSpec((1, 2, 1, LANES), lambda b, c: (b, 0, 0, 0))
    return pl.pallas_call(
        functools.partial(_mlstm_kernel, nc=nc),
        grid=(B, nc),
        in_specs=specs(0) + specs(1) + [pl.BlockSpec((1, LANES), lambda b, c: (0, 0)), cstate, nstate, mstate],
        out_specs=[pl.BlockSpec((1, CHUNK, WC), lambda b, c: (b, c, 0)),
                   pl.BlockSpec((1, CHUNK, WC), lambda b, c: (b, nc - 1 - c, 0)),
                   cstate, nstate, mstate],
        out_shape=[jax.ShapeDtypeStruct((B, L, WC), F32), jax.ShapeDtypeStruct((B, L, WC), F32),
                   jax.ShapeDtypeStruct((B, 2, WC, WC), F32),
                   jax.ShapeDtypeStruct((B, 2, WC, LANES), F32),
                   jax.ShapeDtypeStruct((B, 2, 1, LANES), F32)],
        scratch_shapes=[pltpu.VMEM((2, WC, WC), F32), pltpu.VMEM((2, WC, LANES), F32),
                        pltpu.VMEM((2, 1, LANES), F32)],
        compiler_params=_cparams("arbitrary", "arbitrary"),
        name="mlstm_scan",
    )(proj, proj, proj, proj, proj, proj, proj, proj, gbias, c0, n0, m0)


def _mlstm_state_in(C, n, m):
    if isinstance(C, int):
        return (jnp.zeros((C, 2, WC, WC), F32), jnp.zeros((C, 2, WC, LANES), F32),
                jnp.zeros((C, 2, 1, LANES), F32))
    B = C.shape[0]
    eye = jnp.eye(HC, dtype=F32)
    c0 = jnp.einsum('bdhkv,hg->bdhkgv', C, eye).reshape(B, 2, WC, WC)
    lanes = jnp.stack([jnp.arange(HC) + _fg_lane(d, 0) for d in range(2)])
    onehot = (lanes[:, :, None] == jnp.arange(LANES)[None, None, :]).astype(F32)
    n0 = jnp.einsum('bdhk,dhl->bdhkl', n, onehot).reshape(B, 2, WC, LANES)
    m0 = jnp.einsum('bdh,dhl->bdl', m, onehot).reshape(B, 2, 1, LANES)
    return c0, n0, m0


def _mlstm_state_out(cfin, nfin, mfin):
    B = cfin.shape[0]
    c5 = cfin.reshape(B, 2, HC, DKC, HC, DVC)
    C = jnp.stack([c5[:, :, h, :, h, :] for h in range(HC)], axis=2)
    n4 = nfin.reshape(B, 2, HC, DKC, LANES)
    n = jnp.stack([jnp.stack([n4[:, d, h, :, _fg_lane(d, h)] for h in range(HC)], axis=1) for d in range(2)], axis=1)
    m = jnp.stack([mfin[:, d, 0, _fg_lane(d, 0):_fg_lane(d, 0) + HC] for d in range(2)], axis=1)
    return C, n, m


RT_IDX0 = 0
RT_GATE0 = TOP_K
RT_ROWS = 2 * TOP_K


def _layer_norm_rows(x, g, b):
    mu = jnp.mean(x, axis=-1, keepdims=True)
    xc = x - mu
    var = jnp.mean(xc * xc, axis=-1, keepdims=True)
    return xc * lax.rsqrt(var + LN_EPS) * g + b


def _outproj_kernel(oa_ref, yf_ref, yb_ref, z_ref, hf_ref, hb_ref, om_ref, x_ref, mod_ref, w_ref,
                    sg_ref, mg_ref, lg_ref, lb_ref, rw_ref, rb_ref, x1_ref, h2_ref, rt_ref):
    z = z_ref[0]
    y = (yf_ref[0] + yb_ref[0]) * (z * _sigmoid(z))
    gw = WB // GB
    ob = []
    for g in range(GB):
        seg = y[:, g * gw:(g + 1) * gw]
        ob.append(seg * lax.rsqrt(jnp.mean(seg * seg, axis=-1, keepdims=True) + LN_EPS))
    ob = jnp.concatenate(ob, axis=1) * sg_ref[...]
    hsum = hf_ref[0] + hb_ref[0]
    oc = []
    for h in range(HC):
        seg = hsum[:, h * DVC:(h + 1) * DVC]
        mu = jnp.mean(seg, axis=-1, keepdims=True)
        sc = seg - mu
        oc.append(sc * lax.rsqrt(jnp.mean(sc * sc, axis=-1, keepdims=True) + LN_EPS))
    oc = jnp.concatenate(oc, axis=1) * mg_ref[...] * _sigmoid(om_ref[0])
    mix_in = jnp.concatenate([oa_ref[0], ob, oc], axis=1).astype(BF16)
    mix = jnp.dot(mix_in, w_ref[...], preferred_element_type=F32)
    x1 = _layer_norm_rows(ALPHA * x_ref[0] + mod_ref[0, 2:3, :] * mix, lg_ref[...], lb_ref[...])
    x1_ref[0] = x1
    h2 = x1 * (1.0 + mod_ref[0, 4:5, :]) + mod_ref[0, 3:4, :]
    for j in range(ROW_CH):
        h2_ref[pl.ds(j, h2.shape[0], stride=ROW_CH), :] = h2[:, j * LANES:(j + 1) * LANES]
    h_hi = h2.astype(BF16)
    h_mid = (h2 - h_hi.astype(F32)).astype(BF16)
    logits = jnp.dot(jnp.concatenate([h_hi, h_hi, h_mid], axis=1), rw_ref[...],
                     preferred_element_type=F32) + rb_ref[...]
    lane = lax.broadcasted_iota(jnp.int32, (1, LANES), 1)
    lane_f = lane.astype(F32)
    lg = jnp.where(lane < N_EXPERTS, logits, -jnp.inf)
    vals, ids = [], []
    for _ in range(TOP_K):
        mx = jnp.max(lg, axis=-1, keepdims=True)
        first = jnp.min(jnp.where(lg == mx, lane_f, float(LANES)), axis=-1, keepdims=True)
        vals.append(mx)
        ids.append(first)
        lg = jnp.where(lane_f == first, -jnp.inf, lg)
    es = [jnp.exp(v - vals[0]) for v in vals]
    inv = 1.0 / (es[0] + es[1] + es[2] + es[3])
    rt = jnp.zeros(logits.shape, F32)
    for k in range(TOP_K):
        rt = jnp.where(lane == RT_IDX0 + k, ids[k], rt)
        rt = jnp.where(lane == RT_GATE0 + k, es[k] * inv, rt)
    rt_ref[...] = rt.T[0:RT_ROWS, :]


def _outproj(oa, yf, yb, hf, hb, proj, x, mod, w_out_b, ssd_g, mlstm_g, ln_g, ln_b, router_w, router_b):
    B, L, _ = x.shape
    tl = min(L, 256)
    bm = mod.shape[0]
    mod_idx = (lambda b, i: (b, 0, 0)) if bm > 1 else (lambda b, i: (0, 0, 0))
    row = lambda w: pl.BlockSpec((1, tl, w), lambda b, i: (b, i, 0))
    const = lambda shape: pl.BlockSpec(shape, lambda b, i: (0,) * len(shape))
    rw = jnp.pad(router_w.astype(F32), ((0, 0), (0, LANES - N_EXPERTS)))
    rw_hi = rw.astype(BF16)
    rw_mid = (rw - rw_hi.astype(F32)).astype(BF16)
    rw = jnp.concatenate([rw_hi, rw_mid, rw_hi], axis=0)
    return pl.pallas_call(
        _outproj_kernel,
        grid=(B, L // tl),
        in_specs=[row(WA), row(WB), row(WB),
                  pl.BlockSpec((1, tl, WB), lambda b, i: (b, i, C_Z // WB)),
                  row(WC), row(WC),
                  pl.BlockSpec((1, tl, WC), lambda b, i: (b, i, C_OM // WC)),
                  row(D_MODEL),
                  pl.BlockSpec((1, 6, D_MODEL), mod_idx),
                  const((D_MIX, D_MODEL)), const((1, WB)), const((1, WC)),
                  const((1, D_MODEL)), const((1, D_MODEL)), const((3 * D_MODEL, LANES)), const((1, LANES))],
        out_specs=[row(D_MODEL),
                   pl.BlockSpec((tl * ROW_CH, LANES), lambda b, i: (b * (L // tl) + i, 0)),
                   pl.BlockSpec((RT_ROWS, tl), lambda b, i: (0, b * (L // tl) + i))],
        out_shape=[jax.ShapeDtypeStruct((B, L, D_MODEL), F32),
                   jax.ShapeDtypeStruct((B * L * ROW_CH, LANES), F32),
                   jax.ShapeDtypeStruct((RT_ROWS, B * L), F32)],
        compiler_params=_cparams("arbitrary", "arbitrary"),
        name="outproj_ln_route",
    )(oa, yf, yb, proj, hf, hb, proj, x, mod, w_out_b, ssd_g.reshape(1, WB), mlstm_g.reshape(1, WC),
      ln_g.reshape(1, D_MODEL), ln_b.reshape(1, D_MODEL), rw, _pad_row(router_b))


def _final_ln_kernel(x_ref, ff_ref, mod_ref, g_ref, b_ref, o_ref):
    tl = x_ref.shape[1]
    ff = jnp.concatenate([ff_ref[pl.ds(j, tl, stride=ROW_CH), :] for j in range(ROW_CH)], axis=1)
    o_ref[0] = _layer_norm_rows(ALPHA * x_ref[0] + mod_ref[0, 5:6, :] * ff, g_ref[...], b_ref[...])


def _final_ln(x1, ff, mod, ln_g, ln_b):
    B, L, _ = x1.shape
    tl = min(L, 512)
    bm = mod.shape[0]
    mod_idx = (lambda b, i: (b, 0, 0)) if bm > 1 else (lambda b, i: (0, 0, 0))
    row = pl.BlockSpec((1, tl, D_MODEL), lambda b, i: (b, i, 0))
    const = pl.BlockSpec((1, D_MODEL), lambda b, i: (0, 0))
    return pl.pallas_call(
        _final_ln_kernel,
        grid=(B, L // tl),
        in_specs=[row, pl.BlockSpec((tl * ROW_CH, LANES), lambda b, i: (b * (L // tl) + i, 0)),
                  pl.BlockSpec((1, 6, D_MODEL), mod_idx), const, const],
        out_specs=row,
        out_shape=jax.ShapeDtypeStruct((B, L, D_MODEL), F32),
        compiler_params=_cparams("arbitrary", "arbitrary"),
        name="final_ln",
    )(x1, ff, mod, ln_g.reshape(1, D_MODEL), ln_b.reshape(1, D_MODEL))


MOE_TM = 2048
MOE_R = 128
MOE_U = SUBLANES
MOE_S = MOE_R + 1
MOE_RB = 512
ROW_CH = D_MODEL // LANES
MOE_LMAX = MOE_TM * TOP_K + N_EXPERTS * MOE_U


def _ceil_to_unit(x):
    return jnp.floor((x + (MOE_U - 1.0)) * (1.0 / MOE_U)) * float(MOE_U)


def _build_routing(rt_ref, d_vm, d_sm, g_sm, idx_sm, gate_sm, sems):
    r = rt_ref[...]
    eio = lax.broadcasted_iota(jnp.int32, (LANES, MOE_TM), 0).astype(F32)
    hit = [r[k:k + 1, :] == eio for k in range(TOP_K)]
    sel = jnp.zeros((LANES, MOE_TM), F32)
    for k in range(TOP_K):
        sel = jnp.where(hit[k], 1.0, sel)
    sel_b = sel.astype(BF16)
    tr = lax.broadcasted_iota(jnp.int32, (MOE_RB, MOE_RB), 0)
    tc = lax.broadcasted_iota(jnp.int32, (MOE_RB, MOE_RB), 1)
    before = (tr < tc).astype(BF16)
    run = jnp.zeros((LANES, 1), F32)
    ranks = []
    for c in range(MOE_TM // MOE_RB):
        blk = slice(c * MOE_RB, (c + 1) * MOE_RB)
        ranks.append(jnp.dot(sel_b[:, blk], before, preferred_element_type=F32) + run)
        run = run + jnp.sum(sel[:, blk], axis=1, keepdims=True)
    rank = jnp.concatenate(ranks, axis=1)
    pc_col = _ceil_to_unit(run)
    er = lax.broadcasted_iota(jnp.int32, (LANES, LANES), 0)
    ec = lax.broadcasted_iota(jnp.int32, (LANES, LANES), 1)
    offs_col = jnp.dot((ec < er).astype(F32), jnp.broadcast_to(pc_col, (LANES, LANES)),
                       precision=HIGHEST, preferred_element_type=F32)[:, 0:1]
    cnt_row = lax.dot_general(jnp.ones((SUBLANES, MOE_TM), BF16), sel_b, (((1,), (1,)), ((), ())),
                              preferred_element_type=F32)
    offs_row = jnp.dot(_ceil_to_unit(cnt_row), (er < ec).astype(F32), precision=HIGHEST,
                       preferred_element_type=F32)
    dest = offs_col + rank
    dest4 = [jnp.sum(jnp.where(hit[k], dest, 0.0), axis=0, keepdims=True) for k in range(TOP_K)]
    tail = jnp.concatenate([offs_row[0:RT_ROWS - TOP_K, :],
                            jnp.zeros((RT_ROWS - TOP_K, MOE_TM - LANES), F32)], axis=1)
    d_vm[...] = jnp.concatenate(dest4 + [tail], axis=0).astype(jnp.int32)
    cp_d = pltpu.make_async_copy(d_vm, d_sm, sems.at[0])
    cp_g = pltpu.make_async_copy(rt_ref, g_sm, sems.at[1])
    cp_d.start()
    cp_g.start()
    cp_d.wait()
    cp_g.wait()

    def pad_body(ex, c):
        end = d_sm[TOP_K, ex + 1]
        for u in range(MOE_U):
            pos = jnp.maximum(end - 1 - u, 0)
            idx_sm[pos] = MOE_TM
            gate_sm[pos] = 0.0
        return c

    lax.fori_loop(0, N_EXPERTS, pad_body, 0)

    def place_body(tb, c):
        t0 = pl.multiple_of(tb * LANES, LANES)
        for u in range(LANES):
            for k in range(TOP_K):
                pos = d_sm[k, t0 + u]
                idx_sm[pos] = t0 + u
                gate_sm[pos] = g_sm[RT_GATE0 + k, t0 + u]
        return c

    lax.fori_loop(0, MOE_TM // LANES, place_body, 0)


def _moe_kernel(rt_ref, x_ref, wgu_ref, bgu_ref, wdn_ref, bdn_ref, o_hbm,
                acc, xt, yt, d_vm, d_sm, g_sm, idx_sm, gate_sm, pend_sm, sems):
    i = pl.program_id(0)
    e = pl.program_id(1)

    def gather_pass(start):
        for mi in range(MOE_R):
            t = jnp.minimum(idx_sm[start + mi], MOE_TM - 1)
            src = pl.multiple_of(t * ROW_CH, ROW_CH)
            xt[pl.ds(mi, ROW_CH, stride=MOE_S), :] = x_ref[pl.ds(src, ROW_CH), :]

    def scatter_pass(start, n):
        for gi in range(MOE_R // MOE_U):
            valid = gi * MOE_U < n
            upd = []
            for u in range(MOE_U):
                mi = gi * MOE_U + u
                t = jnp.where(valid, idx_sm[start + mi], MOE_TM)
                gt = jnp.where(valid, gate_sm[start + mi], 0.0)
                dst = pl.multiple_of(t * ROW_CH, ROW_CH)
                upd.append((dst, acc[pl.ds(dst, ROW_CH), :] + gt * yt[pl.ds(mi, ROW_CH, stride=MOE_S), :]))
            for dst, val in upd:
                acc[pl.ds(dst, ROW_CH), :] = val

    @pl.when((i == 0) & (e == 0))
    def _():
        xt[...] = jnp.zeros(xt.shape, F32)
        yt[...] = jnp.zeros(yt.shape, F32)

        def clear(j, c):
            idx_sm[j] = 0
            gate_sm[j] = 0.0
            return c

        lax.fori_loop(0, MOE_LMAX + MOE_R, clear, 0)

    @pl.when(e == 0)
    def _():
        acc[...] = jnp.zeros(acc.shape, F32)
        _build_routing(rt_ref, d_vm, d_sm, g_sm, idx_sm, gate_sm, sems)
        pend_sm[0] = 0
        pend_sm[1] = 0
        gather_pass(0)

    base = d_sm[TOP_K, e]
    end = d_sm[TOP_K, e + 1]
    npass = lax.shift_right_logical(end - base + (MOE_R - 1), MOE_R.bit_length() - 1)

    def pass_body(c, carry):
        r0 = base + c * MOE_R
        x = jnp.concatenate([xt[pl.ds(j * MOE_S, MOE_R), :] for j in range(ROW_CH)], axis=1).astype(BF16)
        scatter_pass(pend_sm[0], pend_sm[1])
        gather_pass(jnp.minimum(r0 + MOE_R, end))
        gu = jnp.dot(x, wgu_ref[0], preferred_element_type=F32) + bgu_ref[0]
        g = jnp.minimum(gu[:, :D_FF], SWIGLU_LIMIT)
        u_ = jnp.clip(gu[:, D_FF:], -SWIGLU_LIMIT, SWIGLU_LIMIT)
        act = g * _sigmoid(SWIGLU_ALPHA * g) * (u_ + 1.0)
        y = jnp.dot(act.astype(BF16), wdn_ref[0], preferred_element_type=F32) + bdn_ref[0]
        for j in range(ROW_CH):
            yt[pl.ds(j * MOE_S, MOE_R), :] = y[:, j * LANES:(j + 1) * LANES]
        pend_sm[0] = r0
        pend_sm[1] = jnp.minimum(MOE_R, end - r0)
        return carry

    lax.fori_loop(0, npass, pass_body, 0)

    @pl.when(e == N_EXPERTS - 1)
    def _():
        scatter_pass(pend_sm[0], pend_sm[1])
        rows = MOE_TM * ROW_CH
        cp = pltpu.make_async_copy(acc.at[pl.ds(0, rows), :],
                                   o_hbm.at[pl.ds(pl.multiple_of(i * rows, rows), rows), :], sems.at[2])
        cp.start()
        cp.wait()


def _moe(xv, rt, w_gu_b, b_gu, w_dn_b, b_dn):
    T = xv.shape[0] // ROW_CH
    nt = T // MOE_TM
    return pl.pallas_call(
        _moe_kernel,
        grid=(nt, N_EXPERTS),
        in_specs=[pl.BlockSpec((RT_ROWS, MOE_TM), lambda i, e: (0, i)),
                  pl.BlockSpec((MOE_TM * ROW_CH, LANES), lambda i, e: (i, 0)),
                  pl.BlockSpec((1, D_MODEL, 2 * D_FF), lambda i, e: (e, 0, 0)),
                  pl.BlockSpec((1, 1, 2 * D_FF), lambda i, e: (e, 0, 0)),
                  pl.BlockSpec((1, D_FF, D_MODEL), lambda i, e: (e, 0, 0)),
                  pl.BlockSpec((1, 1, D_MODEL), lambda i, e: (e, 0, 0))],
        out_specs=pl.BlockSpec(memory_space=pl.ANY),
        scratch_shapes=[pltpu.VMEM(((MOE_TM + 1) * ROW_CH, LANES), F32),
                        pltpu.VMEM((ROW_CH * MOE_S, LANES), F32),
                        pltpu.VMEM((ROW_CH * MOE_S, LANES), F32),
                        pltpu.VMEM((RT_ROWS, MOE_TM), jnp.int32),
                        pltpu.SMEM((RT_ROWS, MOE_TM), jnp.int32),
                        pltpu.SMEM((RT_ROWS, MOE_TM), F32),
                        pltpu.SMEM((MOE_LMAX + MOE_R,), jnp.int32),
                        pltpu.SMEM((MOE_LMAX + MOE_R,), F32),
                        pltpu.SMEM((2,), jnp.int32),
                        pltpu.SemaphoreType.DMA((3,))],
        out_shape=jax.ShapeDtypeStruct((T * ROW_CH, LANES), F32),
        compiler_params=_cparams("arbitrary", "arbitrary"),
        name="moe_ffn",
    )(rt, xv, w_gu_b, b_gu.reshape(N_EXPERTS, 1, 2 * D_FF), w_dn_b, b_dn.reshape(N_EXPERTS, 1, D_MODEL))


def _trunk_layer(x, mod, p, layer, ctx):
    B, L, _ = x.shape
    proj = _inproj(x, mod, p['w_in'], rope=ctx is not None)
    kt, v = _attn_operands(proj, None if ctx is None else ctx['k'], None if ctx is None else ctx['v'])
    oa = _attention(proj, kt, v, p['lam'], p['attn_g'], layer)
    if ctx is None:
        h0 = jnp.zeros((B, 2, NB, WB), F32)
        c0, n0, m0 = _mlstm_state_in(B, None, None)
    else:
        h0 = _ssd_state_in(ctx['ssd'])
        c0, n0, m0 = _mlstm_state_in(ctx['C'], ctx['n'], ctx['m'])
    yf, yb, hfin = _ssd(proj, p['conv_w'], p['conv_b'], p['dt_bias'], p['a_log'], p['d_skip'], h0)
    hf, hb, cfin, nfin, mfin = _mlstm(proj, p['ig_b'], p['fg_b'], c0, n0, m0)
    x1, h2, rt = _outproj(oa, yf, yb, hf, hb, proj, x, mod, p['w_out'], p['ssd_g'], p['mlstm_g'],
                          p['ln1_g'], p['ln1_b'], p['router_w'], p['router_b'])
    ff = _moe(h2, rt, p['w_gu'], p['b_gu'], p['w_dn'], p['b_dn'])
    x2 = _final_ln(x1, ff, mod, p['ln2_g'], p['ln2_b'])
    new_ctx = None
    if ctx is None:
        c_out, n_out, m_out = _mlstm_state_out(cfin, nfin, mfin)
        new_ctx = (proj[:, :, C_KA:C_KA + WA].reshape(B, L, HA, 2 * DQK),
                   proj[:, :, C_VA:C_VA + WA].reshape(B, L, HA, DVA),
                   _ssd_state_out(hfin), c_out, n_out, m_out)
    return x2, new_ctx


def kernel(x_prompt, x_sample, c, cache_attn_k, cache_attn_v, state_ssd, state_mlstm_C, state_mlstm_n, state_mlstm_m, c_ctx, w_mod, b_mod, w_in, lam_q1, lam_k1, lam_q2, lam_k2, attn_g, conv_w, conv_b, dt_bias, a_log, d_skip, ssd_g, ig_b, fg_b, mlstm_g, w_out, ln1_g, ln1_b, router_w, router_b, w_gu, b_gu, w_dn, b_dn, ln2_g, ln2_b):
    nb = c.shape[0]
    cond = jnp.concatenate([c_ctx[None, :], c, jnp.zeros((2 * SUBLANES - 1 - nb, D_MODEL), F32)], axis=0)
    y_prompt, y_sample = x_prompt, x_sample
    outs = [[] for _ in range(6)]
    for l in range(DEPTH):
        p = {'w_in': _permute_w_in(w_in[l]),
             'lam': jnp.stack([lam_q1[l], lam_k1[l], lam_q2[l], lam_k2[l]]),
             'attn_g': attn_g[l], 'conv_w': conv_w[l], 'conv_b': conv_b[l], 'dt_bias': dt_bias[l],
             'a_log': a_log[l], 'd_skip': d_skip[l], 'ssd_g': ssd_g[l], 'ig_b': ig_b[l], 'fg_b': fg_b[l],
             'mlstm_g': mlstm_g[l], 'w_out': w_out[l].astype(BF16), 'ln1_g': ln1_g[l], 'ln1_b': ln1_b[l],
             'router_w': router_w[l], 'router_b': router_b[l], 'w_gu': w_gu[l].astype(BF16), 'b_gu': b_gu[l],
             'w_dn': w_dn[l].astype(BF16), 'b_dn': b_dn[l], 'ln2_g': ln2_g[l], 'ln2_b': ln2_b[l]}
        mod = _modulation(cond, w_mod[l], b_mod[l]).reshape(2 * SUBLANES, 6, D_MODEL)
        y_prompt, st = _trunk_layer(y_prompt, mod[0:1], p, l, None)
        for acc_list, s in zip(outs, st):
            acc_list.append(s)
        ctx = {'k': cache_attn_k[:, l], 'v': cache_attn_v[:, l], 'ssd': state_ssd[:, l],
               'C': state_mlstm_C[:, l], 'n': state_mlstm_n[:, l], 'm': state_mlstm_m[:, l]}
        y_sample, _ = _trunk_layer(y_sample, mod[1:1 + nb], p, l, ctx)
    return (y_prompt, y_sample) + tuple(jnp.stack(o, axis=1) for o in outs)
```

```python
import functools
import math

import jax
import jax.numpy as jnp
import numpy as np
from jax import lax
from jax.experimental import pallas as pl
from jax.experimental.pallas import tpu as pltpu

F32 = jnp.float32
BF16 = jnp.bfloat16
HIGHEST = lax.Precision.HIGHEST

D_MODEL = 1024
DEPTH = 2
GRID_W = 64
HA = 4
DQK = 32
DVA = 2 * DQK
WA = HA * DVA
ROPE_BASE = 10000.0
HB = 8
PB = 64
WB = HB * PB
GB = 2
NB = 64
CONV_W = 3
CONV_CH = WB + 2 * GB * NB
HC = 4
DKC = 64
DVC = 64
WC = HC * DVC
D_MIX = WA + WB + WC
CHUNK = 128
N_EXPERTS = 32
TOP_K = 4
D_FF = D_MODEL
SWIGLU_LIMIT = 7.0
SWIGLU_ALPHA = 1.702
ALPHA = (2 * DEPTH) ** 0.25
LN_EPS = 1e-5

C_QA, C_KA, C_VA, C_OM = 0, 256, 512, 768
C_Z = 1024
C_XBC = 1536
C_QM, C_KM, C_VM = 2304, 2560, 2816
C_DTGM = 3072
N_PROJ = 3200
DT_LANES = 2 * HB
GM_LANE0 = DT_LANES

VMEM_LIMIT = 56 * 1024 * 1024
LANES = 128
SUBLANES = 8


def _cparams(*sem):
    return pltpu.CompilerParams(dimension_semantics=sem, vmem_limit_bytes=VMEM_LIMIT)


def _sigmoid(x):
    return 1.0 / (1.0 + jnp.exp(-x))


def _softplus(x):
    return jnp.maximum(x, 0.0) + jnp.log1p(jnp.exp(-jnp.abs(x)))


def _mod_kernel(c_ref, w_ref, b_ref, o_ref):
    c = c_ref[...]
    s = (c * _sigmoid(c)).astype(BF16)
    o_ref[...] = jnp.dot(s, w_ref[...].astype(BF16), preferred_element_type=F32) + b_ref[...]


def _modulation(cond, w_mod, b_mod):
    rows = cond.shape[0]
    n = w_mod.shape[1]
    tn = D_MODEL
    return pl.pallas_call(
        _mod_kernel,
        grid=(n // tn,),
        in_specs=[pl.BlockSpec((rows, D_MODEL), lambda j: (0, 0)),
                  pl.BlockSpec((D_MODEL, tn), lambda j: (0, j)),
                  pl.BlockSpec((1, tn), lambda j: (0, j))],
        out_specs=pl.BlockSpec((rows, tn), lambda j: (0, j)),
        out_shape=jax.ShapeDtypeStruct((rows, n), F32),
        compiler_params=_cparams("arbitrary"),
        name="modulation",
    )(cond, w_mod, b_mod.reshape(1, n))


def _value_blocks(v):
    lane = lax.broadcasted_iota(jnp.int32, (1, LANES), 1)
    out = []
    for h in range(HA):
        vt = v[:, (h // 2) * LANES:(h // 2 + 1) * LANES]
        if h % 2 == 1:
            vt = pltpu.roll(vt, DVA, 1)
        out.append(jnp.where(lane < DVA, vt, (lane == DVA).astype(F32)).astype(BF16))
    return out


def _inproj_kernel(x_ref, mod_ref, w_ref, cos_ref, sa_ref, sb_ref, *rest, rope):
    o_ref, kt_ref, va_ref = rest[-3:]
    x = x_ref[0]
    sh = mod_ref[0, 0:1, :]
    sc = mod_ref[0, 1:2, :]
    h = (x * (1.0 + sc) + sh).astype(BF16)
    p = jnp.dot(h, w_ref[...], preferred_element_type=F32)

    def rot(t):
        return (t * cos_ref[...] + pltpu.roll(t, WA - DQK // 4, 1) * sa_ref[...]
                + pltpu.roll(t, DQK // 4, 1) * sb_ref[...])

    q = p[:, C_QA:C_QA + WA]
    k = p[:, C_KA:C_KA + WA]
    if rope:
        q = rot(q)
        k = rot(k)
    o_ref[0, :, C_QA:C_QA + WA] = q * (DQK ** -0.5)
    o_ref[0, :, C_KA:C_KA + WA] = k
    o_ref[0, :, C_VA:] = p[:, C_VA:]
    kt_ref[0] = k.T.astype(BF16)
    for h, blk in enumerate(_value_blocks(p[:, C_VA:C_VA + WA])):
        va_ref[0, h] = blk


def _rope_tables(L):
    quarter = DQK // 4
    pos = np.arange(L)
    row = pos // GRID_W
    col = pos % GRID_W
    inv = ROPE_BASE ** (-np.arange(quarter, dtype=np.float32) / quarter)
    lane = np.arange(WA)
    c = lane % DQK
    use_col = (c // (DQK // 2)) == 1
    w = c % (DQK // 2)
    f = w % quarter
    first = w < quarter
    p = jnp.where(use_col[None, :], col[:, None], row[:, None]).astype(F32)
    ang = p * jnp.asarray(inv)[f][None, :]
    cos = jnp.cos(ang)
    sin = jnp.sin(ang)
    sa = jnp.where(first[None, :], -sin, 0.0)
    sb = jnp.where(first[None, :], 0.0, sin)
    return cos, sa, sb


def _inproj(x, mod, w_in_p, rope, ctx_bufs=None):
    B, L, _ = x.shape
    tl = min(L, 512)
    cos, sa, sb = _rope_tables(L)
    bm = mod.shape[0]
    mod_idx = (lambda b, i: (b, 0, 0)) if bm > 1 else (lambda b, i: (0, 0, 0))
    tab = pl.BlockSpec((tl, WA), lambda b, i: (i, 0))
    lk = L if ctx_bufs is None else ctx_bufs[0].shape[2]
    n_in = 6
    return pl.pallas_call(
        functools.partial(_inproj_kernel, rope=rope),
        grid=(B, L // tl),
        in_specs=[pl.BlockSpec((1, tl, D_MODEL), lambda b, i: (b, i, 0)),
                  pl.BlockSpec((1, 6, D_MODEL), mod_idx),
                  pl.BlockSpec((D_MODEL, N_PROJ), lambda b, i: (0, 0)),
                  tab, tab, tab] + ([] if ctx_bufs is None else [pl.BlockSpec(memory_space=pl.ANY)] * 2),
        out_specs=[pl.BlockSpec((1, tl, N_PROJ), lambda b, i: (b, i, 0)),
                   pl.BlockSpec((1, WA, tl), lambda b, i: (b, 0, i)),
                   pl.BlockSpec((1, HA, tl, LANES), lambda b, i: (b, 0, i, 0))],
        out_shape=[jax.ShapeDtypeStruct((B, L, N_PROJ), F32),
                   jax.ShapeDtypeStruct((B, WA, lk), BF16),
                   jax.ShapeDtypeStruct((B, HA, lk, LANES), BF16)],
        input_output_aliases={} if ctx_bufs is None else {n_in: 1, n_in + 1: 2},
        compiler_params=_cparams("arbitrary", "arbitrary"),
        name="inproj",
    )(x, mod, w_in_p, cos, sa, sb, *(() if ctx_bufs is None else ctx_bufs))


def _permute_w_in(w_in):
    sizes = (WA, WA, WA, WB, CONV_CH, 2 * HB, WC, WC, WC, WC, 4 * HC)
    offs = np.concatenate([[0], np.cumsum(sizes)])
    qa, ka, va, z, xbc, dt, qm, km, vm, om, gm = (w_in[:, offs[i]:offs[i + 1]] for i in range(11))
    pad = jnp.zeros((w_in.shape[0], N_PROJ - C_DTGM - 2 * HB - 4 * HC), w_in.dtype)
    return jnp.concatenate([qa, ka, va, om, z, xbc, qm, km, vm, dt, gm, pad], axis=1).astype(BF16)


def _attn_kernel(lamp_ref, q_ref, kt_ref, v_ref, g_ref, o_ref, *, lam_init):
    lp = lamp_ref[...]
    lam = (jnp.exp(jnp.sum(lp[0:1] * lp[1:2], axis=-1, keepdims=True))
           - jnp.exp(jnp.sum(lp[2:3] * lp[3:4], axis=-1, keepdims=True)) + lam_init)
    q = q_ref[0]
    outs = []
    for h in range(HA):
        os_ = []
        for m in range(2):
            c0 = (2 * h + m) * DQK
            s = jnp.dot(q[:, c0:c0 + DQK].astype(BF16), kt_ref[0, c0:c0 + DQK, :],
                        preferred_element_type=F32)
            p = jnp.exp(s - jnp.max(s, axis=-1, keepdims=True)).astype(BF16)
            os_.append(jnp.dot(p, v_ref[0, h], preferred_element_type=F32))
        seg = (os_[0][:, :DVA] * (1.0 / os_[0][:, DVA:DVA + 1])
               - os_[1][:, :DVA] * (lam / os_[1][:, DVA:DVA + 1]))
        ms = jnp.mean(seg * seg, axis=-1, keepdims=True)
        outs.append(seg * lax.rsqrt(ms + LN_EPS))
    o_ref[0] = jnp.concatenate(outs, axis=1) * g_ref[...] * (1.0 - lam_init)


def _attention(proj, kt, va, lam_params, attn_g, layer):
    B, L, _ = proj.shape
    lk = kt.shape[2]
    tq = min(L, 256)
    lam_init = 0.8 - 0.6 * math.exp(-0.3 * layer)
    return pl.pallas_call(
        functools.partial(_attn_kernel, lam_init=lam_init),
        grid=(B, L // tq),
        in_specs=[pl.BlockSpec((4, DQK), lambda b, i: (0, 0)),
                  pl.BlockSpec((1, tq, WA), lambda b, i: (b, i, C_QA // WA)),
                  pl.BlockSpec((1, WA, lk), lambda b, i: (b, 0, 0)),
                  pl.BlockSpec((1, HA, lk, LANES), lambda b, i: (b, 0, 0, 0)),
                  pl.BlockSpec((1, WA), lambda b, i: (0, 0))],
        out_specs=pl.BlockSpec((1, tq, WA), lambda b, i: (b, i, 0)),
        out_shape=jax.ShapeDtypeStruct((B, L, WA), F32),
        compiler_params=_cparams("arbitrary", "arbitrary"),
        name="diff_attention",
    )(lam_params, proj, kt, va, jnp.tile(attn_g, HA).reshape(1, WA))


def _ctx_attn_buffers(ctx_k, ctx_v, L):
    B, P = ctx_k.shape[:2]
    kt = jnp.swapaxes(ctx_k.reshape(B, P, WA), 1, 2).astype(BF16)
    vh = jnp.swapaxes(ctx_v, 1, 2).astype(BF16)
    va = jnp.concatenate([vh, jnp.ones((B, HA, P, 1), BF16), jnp.zeros((B, HA, P, LANES - DVA - 1), BF16)],
                         axis=-1)
    return (jnp.pad(kt, ((0, 0), (0, 0), (L, 0))), jnp.pad(va, ((0, 0), (0, 0), (L, 0), (0, 0))))


def _tri(rev):
    i = lax.broadcasted_iota(jnp.int32, (CHUNK, CHUNK), 0)
    j = lax.broadcasted_iota(jnp.int32, (CHUNK, CHUNK), 1)
    return (j >= i) if rev else (j <= i)


def _split3(x):
    hi = x.astype(BF16)
    r1 = x - hi.astype(F32)
    mid = r1.astype(BF16)
    lo = (r1 - mid.astype(F32)).astype(BF16)
    return hi, mid, lo


def _chunk_cumsum(a, rev):
    tri = _tri(rev).astype(BF16)
    return jnp.dot(jnp.concatenate([tri, tri, tri], axis=1), jnp.concatenate(_split3(a), axis=0),
                   preferred_element_type=F32)


def _widen(t, onehot3):
    return jnp.dot(jnp.concatenate(_split3(t), axis=1), onehot3, preferred_element_type=F32)


def _onehot3(src_lane_of_col, width):
    r = lax.broadcasted_iota(jnp.int32, (3 * LANES, width), 0) % LANES
    c = lax.broadcasted_iota(jnp.int32, (3 * LANES, width), 1)
    return (r == src_lane_of_col(c)).astype(BF16)


def _dir_chunk(c, nc, d):
    return c if d == 0 else nc - 1 - c


def _ssd_dir(x_ref, xp_ref, xn_ref, dt_ref, cw_ref, cb_ref, dtb_ref, alog_ref, ht, chunk, nc, d):
    rev = d == 1
    x = x_ref[0]
    rowid = lax.broadcasted_iota(jnp.int32, (CHUNK, 1), 0)
    prev = jnp.where(chunk == 0, 0.0, xp_ref[0, SUBLANES - 1:SUBLANES, :])
    nxt = jnp.where(chunk == nc - 1, 0.0, xn_ref[0, 0:1, :])
    xm1 = jnp.where(rowid == 0, prev, pltpu.roll(x, 1, 0))
    xp1 = jnp.where(rowid == CHUNK - 1, nxt, pltpu.roll(x, CHUNK - 1, 0))
    xc = cw_ref[0:1, :] * xm1 + cw_ref[1:2, :] * x + cw_ref[2:3, :] * xp1 + cb_ref[...]
    xc = xc * _sigmoid(xc)
    xs = xc[:, :WB]

    lane = lax.broadcasted_iota(jnp.int32, (1, LANES), 1)
    dt = _softplus(dt_ref[0] + dtb_ref[...])
    aneg = jnp.where(lane < DT_LANES, -jnp.exp(alog_ref[...]), 0.0)
    cum = _chunk_cumsum(dt * aneg, rev)
    cum_t = cum.T
    expand = _onehot3(lambda c: d * HB + c // PB, WB)
    dt_w = _widen(dt, expand)
    cum_w = _widen(cum, expand)
    tot_w = cum_w[0:1, :] if rev else cum_w[CHUNK - 1:CHUNK, :]
    e_w = jnp.exp(cum_w)
    te_w = jnp.exp(tot_w - cum_w)
    xin = xs * dt_w
    xw = (xin * te_w).astype(BF16)
    xin_b = xin.astype(BF16)
    mask = _tri(rev)
    ht_b = ht.astype(BF16)
    ys, sts = [], []
    gw = WB // GB
    for g in range(GB):
        bm = xc[:, WB + g * NB:WB + (g + 1) * NB].astype(BF16)
        cm = xc[:, WB + GB * NB + g * NB:WB + GB * NB + (g + 1) * NB].astype(BF16)
        cb = lax.dot_general(cm, bm, (((1,), (1,)), ((), ())), preferred_element_type=F32)
        yd = []
        for r in range(HB // GB):
            h = g * (HB // GB) + r
            hd = d * HB + h
            seg = cum[:, hd:hd + 1] - cum_t[hd:hd + 1, :]
            dec = jnp.exp(jnp.where(mask, seg, -jnp.inf))
            yd.append(jnp.dot((cb * dec).astype(BF16), xin_b[:, h * PB:(h + 1) * PB],
                              preferred_element_type=F32))
        y_off = jnp.dot(cm, ht_b[:, g * gw:(g + 1) * gw], preferred_element_type=F32)
        ys.append(jnp.concatenate(yd, axis=1) + y_off * e_w[:, g * gw:(g + 1) * gw])
        sts.append(lax.dot_general(bm, xw[:, g * gw:(g + 1) * gw], (((0,), (0,)), ((), ())),
                                   preferred_element_type=F32))
    y = jnp.concatenate(ys, axis=1)
    cd = e_w[0:1, :] if rev else e_w[CHUNK - 1:CHUNK, :]
    ht_new = ht * cd + jnp.concatenate(sts, axis=1)
    return y, xs, ht_new


def _ssd_kernel(xf_ref, xfp_ref, xfn_ref, dtf_ref, xb_ref, xbp_ref, xbn_ref, dtb_ref,
                cw_ref, cb_ref, dtbias_ref, alog_ref, dskip_ref, h0_ref,
                yf_ref, yb_ref, hfin_ref, h_scr, *, nc):
    c = pl.program_id(1)

    @pl.when(c == 0)
    def _():
        h_scr[...] = h0_ref[0]

    yf, xs_f, hf = _ssd_dir(xf_ref, xfp_ref, xfn_ref, dtf_ref, cw_ref, cb_ref, dtbias_ref, alog_ref,
                            h_scr[0], c, nc, 0)
    yb, _, hb = _ssd_dir(xb_ref, xbp_ref, xbn_ref, dtb_ref, cw_ref, cb_ref, dtbias_ref, alog_ref,
                         h_scr[1], nc - 1 - c, nc, 1)
    yf_ref[0] = yf + xs_f * dskip_ref[...]
    yb_ref[0] = yb
    h_scr[0] = hf
    h_scr[1] = hb

    @pl.when(c == nc - 1)
    def _():
        hfin_ref[0] = h_scr[...]


def _pad_row(v, width=LANES):
    v = v.reshape(1, -1).astype(F32)
    return jnp.pad(v, ((0, 0), (0, width - v.shape[1])))


def _ssd(proj, conv_w, conv_b, dt_bias, a_log, d_skip, h0):
    B, L, _ = proj.shape
    nc = L // CHUNK
    r8 = CHUNK // SUBLANES
    nb8 = L // SUBLANES
    xblk = C_XBC // CONV_CH
    dblk = C_DTGM // LANES

    def specs(d):
        ch = lambda c: _dir_chunk(c, nc, d)
        return [
            pl.BlockSpec((1, CHUNK, CONV_CH), lambda b, c: (b, ch(c), xblk)),
            pl.BlockSpec((1, SUBLANES, CONV_CH), lambda b, c: (b, jnp.maximum(ch(c) * r8 - 1, 0), xblk)),
            pl.BlockSpec((1, SUBLANES, CONV_CH), lambda b, c: (b, jnp.minimum((ch(c) + 1) * r8, nb8 - 1), xblk)),
            pl.BlockSpec((1, CHUNK, LANES), lambda b, c: (b, ch(c), dblk)),
        ]

    const = lambda shape: pl.BlockSpec(shape, lambda b, c: (0,) * len(shape))
    state = pl.BlockSpec((1, 2, NB, WB), lambda b, c: (b, 0, 0, 0))
    return pl.pallas_call(
        functools.partial(_ssd_kernel, nc=nc),
        grid=(B, nc),
        in_specs=specs(0) + specs(1) + [const((CONV_W, CONV_CH)), const((1, CONV_CH)), const((1, LANES)),
                                        const((1, LANES)), const((1, WB)), state],
        out_specs=[pl.BlockSpec((1, CHUNK, WB), lambda b, c: (b, c, 0)),
                   pl.BlockSpec((1, CHUNK, WB), lambda b, c: (b, nc - 1 - c, 0)),
                   state],
        out_shape=[jax.ShapeDtypeStruct((B, L, WB), F32), jax.ShapeDtypeStruct((B, L, WB), F32),
                   jax.ShapeDtypeStruct((B, 2, NB, WB), F32)],
        scratch_shapes=[pltpu.VMEM((2, NB, WB), F32)],
        compiler_params=_cparams("arbitrary", "arbitrary"),
        name="ssd_scan",
    )(proj, proj, proj, proj, proj, proj, proj, proj,
      conv_w, conv_b.reshape(1, CONV_CH), _pad_row(dt_bias), _pad_row(a_log),
      jnp.repeat(d_skip, PB).reshape(1, WB), h0)


def _ssd_state_in(state):
    B = state.shape[0]
    return jnp.transpose(state, (0, 1, 4, 2, 3)).reshape(B, 2, NB, WB)


def _ssd_state_out(ht):
    B = ht.shape[0]
    return jnp.transpose(ht.reshape(B, 2, NB, HB, PB), (0, 1, 3, 4, 2))


def _fg_lane(d, h):
    return GM_LANE0 + d * 2 * HC + HC + h


def _mlstm_dir(q_ref, k_ref, v_ref, g_ref, gbias_ref, cbd, nmat, m_prev, d):
    rev = d == 1
    q_b = q_ref[0].astype(BF16)
    k = k_ref[0] * (DKC ** -0.5)
    k_b = k.astype(BF16)
    v_b = v_ref[0].astype(BF16)
    pre = g_ref[0] + gbias_ref[...]
    bc = _chunk_cumsum(-_softplus(-pre), rev)
    ig = pltpu.roll(pre, HC, 1)
    tot = bc[0:1, :] if rev else bc[CHUNK - 1:CHUNK, :]
    gend = tot - bc + ig
    mloc = jnp.max(gend, axis=0, keepdims=True)
    wend = jnp.exp(gend - mloc)
    m_new = jnp.maximum(tot + m_prev, mloc)
    a_prev = jnp.exp(tot + m_prev - m_new)
    a_loc = jnp.exp(mloc - m_new)
    inter = bc + m_prev
    bc_t = bc.T
    ig_t = ig.T
    lf0 = _fg_lane(d, 0)
    to_tile = _onehot3(lambda c: lf0 + c // LANES, HC * LANES)
    to_head = _onehot3(lambda c: lf0 + c // DVC, WC)
    lane = lax.broadcasted_iota(jnp.int32, (1, LANES), 1)
    mask = _tri(rev)
    bc_w = _widen(bc, to_tile)
    dms = []
    rmax = jnp.zeros((CHUNK, LANES), F32)
    for h in range(HC):
        lf = lf0 + h
        dm = jnp.where(mask, bc_w[:, h * LANES:(h + 1) * LANES] - bc_t[lf:lf + 1, :] + ig_t[lf:lf + 1, :],
                       -jnp.inf)
        dms.append(dm)
        rmax = jnp.where(lane == lf, jnp.max(dm, axis=-1, keepdims=True), rmax)
    mt = jnp.maximum(inter, rmax)
    mt_w = _widen(mt, to_tile)
    w_inter = jnp.exp(inter - mt)
    nums = []
    den_in = jnp.zeros((CHUNK, LANES), F32)
    for h in range(HC):
        hk = slice(h * DKC, (h + 1) * DKC)
        qk = lax.dot_general(q_b[:, hk], k_b[:, hk], (((1,), (1,)), ((), ())), preferred_element_type=F32)
        qk = qk * jnp.exp(dms[h] - mt_w[:, h * LANES:(h + 1) * LANES])
        nums.append(jnp.dot(qk.astype(BF16), v_b[:, h * DVC:(h + 1) * DVC], preferred_element_type=F32))
        den_in = jnp.where(lane == lf0 + h, jnp.sum(qk, axis=-1, keepdims=True), den_in)
    cross_num = jnp.dot(q_b, cbd.astype(BF16), preferred_element_type=F32)
    cross_den = jnp.dot(q_b, nmat.astype(BF16), preferred_element_type=F32)
    den = den_in + w_inter * cross_den
    head_lane = (lane >= lf0) & (lane < lf0 + HC)
    rinv = jnp.where(head_lane, 1.0 / jnp.maximum(jnp.abs(den), jnp.exp(-mt)), 0.0)
    h_out = (jnp.concatenate(nums, axis=1) + _widen(w_inter, to_head) * cross_num) * _widen(rinv, to_head)
    kw = (k * _widen(wend, to_head)).astype(BF16)
    tn = (((0,), (0,)), ((), ()))
    rr = lax.broadcasted_iota(jnp.int32, (WC, WC), 0) // DKC
    cc = lax.broadcasted_iota(jnp.int32, (WC, WC), 1) // DVC
    c_loc = jnp.where(rr == cc, lax.dot_general(kw, v_b, tn, preferred_element_type=F32), 0.0)
    nr = lax.broadcasted_iota(jnp.int32, (WC, LANES), 0) // DKC
    nl = lax.broadcasted_iota(jnp.int32, (WC, LANES), 1)
    n_loc = jnp.where(nl == lf0 + nr,
                      lax.dot_general(kw, jnp.ones((CHUNK, LANES), BF16), tn, preferred_element_type=F32), 0.0)
    cbd_new = cbd * _widen(a_prev, to_head) + c_loc * _widen(a_loc, to_head)
    nmat_new = nmat * a_prev + n_loc * a_loc
    return h_out, cbd_new, nmat_new, m_new


def _mlstm_kernel(qf_ref, kf_ref, vf_ref, gf_ref, qb_ref, kb_ref, vb_ref, gb_ref, gbias_ref,
                  c0_ref, n0_ref, m0_ref, hf_ref, hb_ref, cfin_ref, nfin_ref, mfin_ref,
                  c_scr, n_scr, m_scr, *, nc):
    c = pl.program_id(1)

    @pl.when(c == 0)
    def _():
        c_scr[...] = c0_ref[0]
        n_scr[...] = n0_ref[0]
        m_scr[...] = m0_ref[0]

    hf, cf, nf, mf = _mlstm_dir(qf_ref, kf_ref, vf_ref, gf_ref, gbias_ref, c_scr[0], n_scr[0], m_scr[0], 0)
    hb, cb, nb_, mb = _mlstm_dir(qb_ref, kb_ref, vb_ref, gb_ref, gbias_ref, c_scr[1], n_scr[1], m_scr[1], 1)
    hf_ref[0] = hf
    hb_ref[0] = hb
    c_scr[0] = cf
    c_scr[1] = cb
    n_scr[0] = nf
    n_scr[1] = nb_
    m_scr[0] = mf
    m_scr[1] = mb

    @pl.when(c == nc - 1)
    def _():
        cfin_ref[0] = c_scr[...]
        nfin_ref[0] = n_scr[...]
        mfin_ref[0] = m_scr[...]


def _mlstm(proj, ig_b, fg_b, c0, n0, m0):
    B, L, _ = proj.shape
    nc = L // CHUNK
    gbias = jnp.concatenate([ig_b[0], fg_b[0], ig_b[1], fg_b[1]]).astype(F32)
    gbias = jnp.pad(gbias, (GM_LANE0, LANES - GM_LANE0 - 4 * HC)).reshape(1, LANES)

    def specs(d):
        ch = lambda c: _dir_chunk(c, nc, d)
        col = lambda off: pl.BlockSpec((1, CHUNK, WC), lambda b, c: (b, ch(c), off // WC))
        return [col(C_QM), col(C_KM), col(C_VM),
                pl.BlockSpec((1, CHUNK, LANES), lambda b, c: (b, ch(c), C_DTGM // LANES))]

    cstate = pl.BlockSpec((1, 2, WC, WC), lambda b, c: (b, 0, 0, 0))
    nstate = pl.BlockSpec((1, 2, WC, LANES), lambda b, c: (b, 0, 0, 0))
    mstate = pl.BlockSpec((1, 2, 1, LANES), lambda b, c: (b, 0, 0, 0))
    return pl.pallas_call(
        functools.partial(_mlstm_kernel, nc=nc),
        grid=(B, nc),
        in_specs=specs(0) + specs(1) + [pl.BlockSpec((1, LANES), lambda b, c: (0, 0)), cstate, nstate, mstate],
        out_specs=[pl.BlockSpec((1, CHUNK, WC), lambda b, c: (b, c, 0)),
                   pl.BlockSpec((1, CHUNK, WC), lambda b, c: (b, nc - 1 - c, 0)),
                   cstate, nstate, mstate],
        out_shape=[jax.ShapeDtypeStruct((B, L, WC), F32), jax.ShapeDtypeStruct((B, L, WC), F32),
                   jax.ShapeDtypeStruct((B, 2, WC, WC), F32),
                   jax.ShapeDtypeStruct((B, 2, WC, LANES), F32),
                   jax.ShapeDtypeStruct((B, 2, 1, LANES), F32)],
        scratch_shapes=[pltpu.VMEM((2, WC, WC), F32), pltpu.VMEM((2, WC, LANES), F32),
                        pltpu.VMEM((2, 1, LANES), F32)],
        compiler_params=_cparams("arbitrary", "arbitrary"),
        name="mlstm_scan",
    )(proj, proj, proj, proj, proj, proj, proj, proj, gbias, c0, n0, m0)


def _mlstm_state_in(C, n, m):
    if isinstance(C, int):
        return (jnp.zeros((C, 2, WC, WC), F32), jnp.zeros((C, 2, WC, LANES), F32),
                jnp.zeros((C, 2, 1, LANES), F32))
    B = C.shape[0]
    eye = jnp.eye(HC, dtype=F32)
    c0 = jnp.einsum('bdhkv,hg->bdhkgv', C, eye).reshape(B, 2, WC, WC)
    lanes = jnp.stack([jnp.arange(HC) + _fg_lane(d, 0) for d in range(2)])
    onehot = (lanes[:, :, None] == jnp.arange(LANES)[None, None, :]).astype(F32)
    n0 = jnp.einsum('bdhk,dhl->bdhkl', n, onehot).reshape(B, 2, WC, LANES)
    m0 = jnp.einsum('bdh,dhl->bdl', m, onehot).reshape(B, 2, 1, LANES)
    return c0, n0, m0


def _mlstm_state_out(cfin, nfin, mfin):
    B = cfin.shape[0]
    c5 = cfin.reshape(B, 2, HC, DKC, HC, DVC)
    C = jnp.stack([c5[:, :, h, :, h, :] for h in range(HC)], axis=2)
    n4 = nfin.reshape(B, 2, HC, DKC, LANES)
    n = jnp.stack([jnp.stack([n4[:, d, h, :, _fg_lane(d, h)] for h in range(HC)], axis=1) for d in range(2)], axis=1)
    m = jnp.stack([mfin[:, d, 0, _fg_lane(d, 0):_fg_lane(d, 0) + HC] for d in range(2)], axis=1)
    return C, n, m


RT_IDX0 = 0
RT_GATE0 = TOP_K
RT_ROWS = 2 * TOP_K


def _layer_norm_rows(x, g, b):
    mu = jnp.mean(x, axis=-1, keepdims=True)
    xc = x - mu
    var = jnp.mean(xc * xc, axis=-1, keepdims=True)
    return xc * lax.rsqrt(var + LN_EPS) * g + b


def _outproj_kernel(oa_ref, yf_ref, yb_ref, z_ref, hf_ref, hb_ref, om_ref, x_ref, mod_ref, w_ref,
                    sg_ref, mg_ref, lg_ref, lb_ref, rw_ref, rb_ref, x1_ref, h2_ref, rt_ref):
    z = z_ref[0]
    y = (yf_ref[0] + yb_ref[0]) * (z * _sigmoid(z))
    gw = WB // GB
    ob = []
    for g in range(GB):
        seg = y[:, g * gw:(g + 1) * gw]
        ob.append(seg * lax.rsqrt(jnp.mean(seg * seg, axis=-1, keepdims=True) + LN_EPS))
    ob = jnp.concatenate(ob, axis=1) * sg_ref[...]
    hsum = hf_ref[0] + hb_ref[0]
    oc = []
    for h in range(HC):
        seg = hsum[:, h * DVC:(h + 1) * DVC]
        mu = jnp.mean(seg, axis=-1, keepdims=True)
        sc = seg - mu
        oc.append(sc * lax.rsqrt(jnp.mean(sc * sc, axis=-1, keepdims=True) + LN_EPS))
    oc = jnp.concatenate(oc, axis=1) * mg_ref[...] * _sigmoid(om_ref[0])
    mix_in = jnp.concatenate([oa_ref[0], ob, oc], axis=1).astype(BF16)
    mix = jnp.dot(mix_in, w_ref[...], preferred_element_type=F32)
    x1 = _layer_norm_rows(ALPHA * x_ref[0] + mod_ref[0, 2:3, :] * mix, lg_ref[...], lb_ref[...])
    x1_ref[0] = x1
    h2 = x1 * (1.0 + mod_ref[0, 4:5, :]) + mod_ref[0, 3:4, :]
    for j in range(ROW_CH):
        h2_ref[pl.ds(j, h2.shape[0], stride=ROW_CH), :] = h2[:, j * LANES:(j + 1) * LANES]
    h_hi = h2.astype(BF16)
    h_mid = (h2 - h_hi.astype(F32)).astype(BF16)
    logits = jnp.dot(jnp.concatenate([h_hi, h_hi, h_mid], axis=1), rw_ref[...],
                     preferred_element_type=F32) + rb_ref[...]
    lane = lax.broadcasted_iota(jnp.int32, (1, LANES), 1)
    lane_f = lane.astype(F32)
    lg = jnp.where(lane < N_EXPERTS, logits, -jnp.inf)
    vals, ids = [], []
    for _ in range(TOP_K):
        mx = jnp.max(lg, axis=-1, keepdims=True)
        first = jnp.min(jnp.where(lg == mx, lane_f, float(LANES)), axis=-1, keepdims=True)
        vals.append(mx)
        ids.append(first)
        lg = jnp.where(lane_f == first, -jnp.inf, lg)
    es = [jnp.exp(v - vals[0]) for v in vals]
    inv = 1.0 / (es[0] + es[1] + es[2] + es[3])
    rt = jnp.zeros(logits.shape, F32)
    for k in range(TOP_K):
        rt = jnp.where(lane == RT_IDX0 + k, ids[k], rt)
        rt = jnp.where(lane == RT_GATE0 + k, es[k] * inv, rt)
    rt_ref[...] = rt.T[0:RT_ROWS, :]


def _outproj(oa, yf, yb, hf, hb, proj, x, mod, w_out_b, ssd_g, mlstm_g, ln_g, ln_b, router_w, router_b):
    B, L, _ = x.shape
    tl = min(L, 256)
    bm = mod.shape[0]
    mod_idx = (lambda b, i: (b, 0, 0)) if bm > 1 else (lambda b, i: (0, 0, 0))
    row = lambda w: pl.BlockSpec((1, tl, w), lambda b, i: (b, i, 0))
    const = lambda shape: pl.BlockSpec(shape, lambda b, i: (0,) * len(shape))
    rw = jnp.pad(router_w.astype(F32), ((0, 0), (0, LANES - N_EXPERTS)))
    rw_hi = rw.astype(BF16)
    rw_mid = (rw - rw_hi.astype(F32)).astype(BF16)
    rw = jnp.concatenate([rw_hi, rw_mid, rw_hi], axis=0)
    return pl.pallas_call(
        _outproj_kernel,
        grid=(B, L // tl),
        in_specs=[row(WA), row(WB), row(WB),
                  pl.BlockSpec((1, tl, WB), lambda b, i: (b, i, C_Z // WB)),
                  row(WC), row(WC),
                  pl.BlockSpec((1, tl, WC), lambda b, i: (b, i, C_OM // WC)),
                  row(D_MODEL),
                  pl.BlockSpec((1, 6, D_MODEL), mod_idx),
                  const((D_MIX, D_MODEL)), const((1, WB)), const((1, WC)),
                  const((1, D_MODEL)), const((1, D_MODEL)), const((3 * D_MODEL, LANES)), const((1, LANES))],
        out_specs=[row(D_MODEL),
                   pl.BlockSpec((tl * ROW_CH, LANES), lambda b, i: (b * (L // tl) + i, 0)),
                   pl.BlockSpec((RT_ROWS, tl), lambda b, i: (0, b * (L // tl) + i))],
        out_shape=[jax.ShapeDtypeStruct((B, L, D_MODEL), F32),
                   jax.ShapeDtypeStruct((B * L * ROW_CH, LANES), F32),
                   jax.ShapeDtypeStruct((RT_ROWS, B * L), F32)],
        compiler_params=_cparams("arbitrary", "arbitrary"),
        name="outproj_ln_route",
    )(oa, yf, yb, proj, hf, hb, proj, x, mod, w_out_b, ssd_g.reshape(1, WB), mlstm_g.reshape(1, WC),
      ln_g.reshape(1, D_MODEL), ln_b.reshape(1, D_MODEL), rw, _pad_row(router_b))


def _final_ln_kernel(x_ref, ff_ref, mod_ref, g_ref, b_ref, o_ref):
    tl = x_ref.shape[1]
    ff = jnp.concatenate([ff_ref[pl.ds(j, tl, stride=ROW_CH), :] for j in range(ROW_CH)], axis=1)
    o_ref[0] = _layer_norm_rows(ALPHA * x_ref[0] + mod_ref[0, 5:6, :] * ff, g_ref[...], b_ref[...])


def _final_ln(x1, ff, mod, ln_g, ln_b):
    B, L, _ = x1.shape
    tl = min(L, 512)
    bm = mod.shape[0]
    mod_idx = (lambda b, i: (b, 0, 0)) if bm > 1 else (lambda b, i: (0, 0, 0))
    row = pl.BlockSpec((1, tl, D_MODEL), lambda b, i: (b, i, 0))
    const = pl.BlockSpec((1, D_MODEL), lambda b, i: (0, 0))
    return pl.pallas_call(
        _final_ln_kernel,
        grid=(B, L // tl),
        in_specs=[row, pl.BlockSpec((tl * ROW_CH, LANES), lambda b, i: (b * (L // tl) + i, 0)),
                  pl.BlockSpec((1, 6, D_MODEL), mod_idx), const, const],
        out_specs=row,
        out_shape=jax.ShapeDtypeStruct((B, L, D_MODEL), F32),
        compiler_params=_cparams("arbitrary", "arbitrary"),
        name="final_ln",
    )(x1, ff, mod, ln_g.reshape(1, D_MODEL), ln_b.reshape(1, D_MODEL))


MOE_TM = 2048
MOE_R = 128
MOE_U = SUBLANES
MOE_S = MOE_R + 1
MOE_RB = 512
ROW_CH = D_MODEL // LANES
MOE_LMAX = MOE_TM * TOP_K + N_EXPERTS * MOE_U


def _ceil_to_unit(x):
    return jnp.floor((x + (MOE_U - 1.0)) * (1.0 / MOE_U)) * float(MOE_U)


def _build_routing(rt_ref, d_vm, d_sm, g_sm, idx_sm, gate_sm, sems):
    r = rt_ref[...]
    eio = lax.broadcasted_iota(jnp.int32, (LANES, MOE_TM), 0).astype(F32)
    hit = [r[k:k + 1, :] == eio for k in range(TOP_K)]
    sel = jnp.zeros((LANES, MOE_TM), F32)
    for k in range(TOP_K):
        sel = jnp.where(hit[k], 1.0, sel)
    sel_b = sel.astype(BF16)
    tr = lax.broadcasted_iota(jnp.int32, (MOE_RB, MOE_RB), 0)
    tc = lax.broadcasted_iota(jnp.int32, (MOE_RB, MOE_RB), 1)
    before = (tr < tc).astype(BF16)
    run = jnp.zeros((LANES, 1), F32)
    ranks = []
    for c in range(MOE_TM // MOE_RB):
        blk = slice(c * MOE_RB, (c + 1) * MOE_RB)
        ranks.append(jnp.dot(sel_b[:, blk], before, preferred_element_type=F32) + run)
        run = run + jnp.sum(sel[:, blk], axis=1, keepdims=True)
    rank = jnp.concatenate(ranks, axis=1)
    pc_col = _ceil_to_unit(run)
    er = lax.broadcasted_iota(jnp.int32, (LANES, LANES), 0)
    ec = lax.broadcasted_iota(jnp.int32, (LANES, LANES), 1)
    offs_col = jnp.dot((ec < er).astype(F32), jnp.broadcast_to(pc_col, (LANES, LANES)),
                       precision=HIGHEST, preferred_element_type=F32)[:, 0:1]
    cnt_row = lax.dot_general(jnp.ones((SUBLANES, MOE_TM), BF16), sel_b, (((1,), (1,)), ((), ())),
                              preferred_element_type=F32)
    offs_row = jnp.dot(_ceil_to_unit(cnt_row), (er < ec).astype(F32), precision=HIGHEST,
                       preferred_element_type=F32)
    dest = offs_col + rank
    dest4 = [jnp.sum(jnp.where(hit[k], dest, 0.0), axis=0, keepdims=True) for k in range(TOP_K)]
    tail = jnp.concatenate([offs_row[0:RT_ROWS - TOP_K, :],
                            jnp.zeros((RT_ROWS - TOP_K, MOE_TM - LANES), F32)], axis=1)
    d_vm[...] = jnp.concatenate(dest4 + [tail], axis=0).astype(jnp.int32)
    cp_d = pltpu.make_async_copy(d_vm, d_sm, sems.at[0])
    cp_g = pltpu.make_async_copy(rt_ref, g_sm, sems.at[1])
    cp_d.start()
    cp_g.start()
    cp_d.wait()
    cp_g.wait()

    def pad_body(ex, c):
        end = d_sm[TOP_K, ex + 1]
        for u in range(MOE_U):
            pos = jnp.maximum(end - 1 - u, 0)
            idx_sm[pos] = MOE_TM
            gate_sm[pos] = 0.0
        return c

    lax.fori_loop(0, N_EXPERTS, pad_body, 0)

    def place_body(tb, c):
        t0 = pl.multiple_of(tb * LANES, LANES)
        for u in range(LANES):
            for k in range(TOP_K):
                pos = d_sm[k, t0 + u]
                idx_sm[pos] = t0 + u
                gate_sm[pos] = g_sm[RT_GATE0 + k, t0 + u]
        return c

    lax.fori_loop(0, MOE_TM // LANES, place_body, 0)


def _moe_kernel(rt_ref, x_ref, wgu_ref, bgu_ref, wdn_ref, bdn_ref, o_hbm,
                acc, xt, yt, d_vm, d_sm, g_sm, idx_sm, gate_sm, pend_sm, sems):
    i = pl.program_id(0)
    e = pl.program_id(1)

    def gather_pass(start):
        for mi in range(MOE_R):
            t = jnp.minimum(idx_sm[start + mi], MOE_TM - 1)
            src = pl.multiple_of(t * ROW_CH, ROW_CH)
            xt[pl.ds(mi, ROW_CH, stride=MOE_S), :] = x_ref[pl.ds(src, ROW_CH), :]

    def scatter_pass(start, n):
        for gi in range(MOE_R // MOE_U):
            valid = gi * MOE_U < n
            upd = []
            for u in range(MOE_U):
                mi = gi * MOE_U + u
                t = jnp.where(valid, idx_sm[start + mi], MOE_TM)
                gt = jnp.where(valid, gate_sm[start + mi], 0.0)
                dst = pl.multiple_of(t * ROW_CH, ROW_CH)
                upd.append((dst, acc[pl.ds(dst, ROW_CH), :] + gt * yt[pl.ds(mi, ROW_CH, stride=MOE_S), :]))
            for dst, val in upd:
                acc[pl.ds(dst, ROW_CH), :] = val

    @pl.when((i == 0) & (e == 0))
    def _():
        xt[...] = jnp.zeros(xt.shape, F32)
        yt[...] = jnp.zeros(yt.shape, F32)

        def clear(j, c):
            idx_sm[j] = 0
            gate_sm[j] = 0.0
            return c

        lax.fori_loop(0, MOE_LMAX + MOE_R, clear, 0)

    @pl.when(e == 0)
    def _():
        acc[...] = jnp.zeros(acc.shape, F32)
        _build_routing(rt_ref, d_vm, d_sm, g_sm, idx_sm, gate_sm, sems)
        pend_sm[0] = 0
        pend_sm[1] = 0
        gather_pass(0)

    base = d_sm[TOP_K, e]
    end = d_sm[TOP_K, e + 1]
    npass = lax.shift_right_logical(end - base + (MOE_R - 1), MOE_R.bit_length() - 1)

    def pass_body(c, carry):
        r0 = base + c * MOE_R
        x = jnp.concatenate([xt[pl.ds(j * MOE_S, MOE_R), :] for j in range(ROW_CH)], axis=1).astype(BF16)
        scatter_pass(pend_sm[0], pend_sm[1])
        gather_pass(jnp.minimum(r0 + MOE_R, end))
        gu = jnp.dot(x, wgu_ref[0], preferred_element_type=F32) + bgu_ref[0]
        g = jnp.minimum(gu[:, :D_FF], SWIGLU_LIMIT)
        u_ = jnp.clip(gu[:, D_FF:], -SWIGLU_LIMIT, SWIGLU_LIMIT)
        act = g * _sigmoid(SWIGLU_ALPHA * g) * (u_ + 1.0)
        y = jnp.dot(act.astype(BF16), wdn_ref[0], preferred_element_type=F32) + bdn_ref[0]
        for j in range(ROW_CH):
            yt[pl.ds(j * MOE_S, MOE_R), :] = y[:, j * LANES:(j + 1) * LANES]
        pend_sm[0] = r0
        pend_sm[1] = jnp.minimum(MOE_R, end - r0)
        return carry

    lax.fori_loop(0, npass, pass_body, 0)

    @pl.when(e == N_EXPERTS - 1)
    def _():
        scatter_pass(pend_sm[0], pend_sm[1])
        rows = MOE_TM * ROW_CH
        cp = pltpu.make_async_copy(acc.at[pl.ds(0, rows), :],
                                   o_hbm.at[pl.ds(pl.multiple_of(i * rows, rows), rows), :], sems.at[2])
        cp.start()
        cp.wait()


def _moe(xv, rt, w_gu_b, b_gu, w_dn_b, b_dn):
    T = xv.shape[0] // ROW_CH
    nt = T // MOE_TM
    return pl.pallas_call(
        _moe_kernel,
        grid=(nt, N_EXPERTS),
        in_specs=[pl.BlockSpec((RT_ROWS, MOE_TM), lambda i, e: (0, i)),
                  pl.BlockSpec((MOE_TM * ROW_CH, LANES), lambda i, e: (i, 0)),
                  pl.BlockSpec((1, D_MODEL, 2 * D_FF), lambda i, e: (e, 0, 0)),
                  pl.BlockSpec((1, 1, 2 * D_FF), lambda i, e: (e, 0, 0)),
                  pl.BlockSpec((1, D_FF, D_MODEL), lambda i, e: (e, 0, 0)),
                  pl.BlockSpec((1, 1, D_MODEL), lambda i, e: (e, 0, 0))],
        out_specs=pl.BlockSpec(memory_space=pl.ANY),
        scratch_shapes=[pltpu.VMEM(((MOE_TM + 1) * ROW_CH, LANES), F32),
                        pltpu.VMEM((ROW_CH * MOE_S, LANES), F32),
                        pltpu.VMEM((ROW_CH * MOE_S, LANES), F32),
                        pltpu.VMEM((RT_ROWS, MOE_TM), jnp.int32),
                        pltpu.SMEM((RT_ROWS, MOE_TM), jnp.int32),
                        pltpu.SMEM((RT_ROWS, MOE_TM), F32),
                        pltpu.SMEM((MOE_LMAX + MOE_R,), jnp.int32),
                        pltpu.SMEM((MOE_LMAX + MOE_R,), F32),
                        pltpu.SMEM((2,), jnp.int32),
                        pltpu.SemaphoreType.DMA((3,))],
        out_shape=jax.ShapeDtypeStruct((T * ROW_CH, LANES), F32),
        compiler_params=_cparams("arbitrary", "arbitrary"),
        name="moe_ffn",
    )(rt, xv, w_gu_b, b_gu.reshape(N_EXPERTS, 1, 2 * D_FF), w_dn_b, b_dn.reshape(N_EXPERTS, 1, D_MODEL))


def _trunk_layer(x, mod, p, layer, ctx):
    B, L, _ = x.shape
    ctx_bufs = None if ctx is None else _ctx_attn_buffers(ctx['k'], ctx['v'], L)
    proj, kt, va = _inproj(x, mod, p['w_in'], rope=ctx is not None, ctx_bufs=ctx_bufs)
    oa = _attention(proj, kt, va, p['lam'], p['attn_g'], layer)
    if ctx is None:
        h0 = jnp.zeros((B, 2, NB, WB), F32)
        c0, n0, m0 = _mlstm_state_in(B, None, None)
    else:
        h0 = _ssd_state_in(ctx['ssd'])
        c0, n0, m0 = _mlstm_state_in(ctx['C'], ctx['n'], ctx['m'])
    yf, yb, hfin = _ssd(proj, p['conv_w'], p['conv_b'], p['dt_bias'], p['a_log'], p['d_skip'], h0)
    hf, hb, cfin, nfin, mfin = _mlstm(proj, p['ig_b'], p['fg_b'], c0, n0, m0)
    x1, h2, rt = _outproj(oa, yf, yb, hf, hb, proj, x, mod, p['w_out'], p['ssd_g'], p['mlstm_g'],
                          p['ln1_g'], p['ln1_b'], p['router_w'], p['router_b'])
    ff = _moe(h2, rt, p['w_gu'], p['b_gu'], p['w_dn'], p['b_dn'])
    x2 = _final_ln(x1, ff, mod, p['ln2_g'], p['ln2_b'])
    new_ctx = None
    if ctx is None:
        c_out, n_out, m_out = _mlstm_state_out(cfin, nfin, mfin)
        new_ctx = (proj[:, :, C_KA:C_KA + WA].reshape(B, L, HA, 2 * DQK),
                   proj[:, :, C_VA:C_VA + WA].reshape(B, L, HA, DVA),
                   _ssd_state_out(hfin), c_out, n_out, m_out)
    return x2, new_ctx


def kernel(x_prompt, x_sample, c, cache_attn_k, cache_attn_v, state_ssd, state_mlstm_C, state_mlstm_n, state_mlstm_m, c_ctx, w_mod, b_mod, w_in, lam_q1, lam_k1, lam_q2, lam_k2, attn_g, conv_w, conv_b, dt_bias, a_log, d_skip, ssd_g, ig_b, fg_b, mlstm_g, w_out, ln1_g, ln1_b, router_w, router_b, w_gu, b_gu, w_dn, b_dn, ln2_g, ln2_b):
    nb = c.shape[0]
    cond = jnp.concatenate([c_ctx[None, :], c, jnp.zeros((2 * SUBLANES - 1 - nb, D_MODEL), F32)], axis=0)
    y_prompt, y_sample = x_prompt, x_sample
    outs = [[] for _ in range(6)]
    for l in range(DEPTH):
        p = {'w_in': _permute_w_in(w_in[l]),
             'lam': jnp.stack([lam_q1[l], lam_k1[l], lam_q2[l], lam_k2[l]]),
             'attn_g': attn_g[l], 'conv_w': conv_w[l], 'conv_b': conv_b[l], 'dt_bias': dt_bias[l],
             'a_log': a_log[l], 'd_skip': d_skip[l], 'ssd_g': ssd_g[l], 'ig_b': ig_b[l], 'fg_b': fg_b[l],
             'mlstm_g': mlstm_g[l], 'w_out': w_out[l].astype(BF16), 'ln1_g': ln1_g[l], 'ln1_b': ln1_b[l],
             'router_w': router_w[l], 'router_b': router_b[l], 'w_gu': w_gu[l].astype(BF16), 'b_gu': b_gu[l],
             'w_dn': w_dn[l].astype(BF16), 'b_dn': b_dn[l], 'ln2_g': ln2_g[l], 'ln2_b': ln2_b[l]}
        mod = _modulation(cond, w_mod[l], b_mod[l]).reshape(2 * SUBLANES, 6, D_MODEL)
        y_prompt, st = _trunk_layer(y_prompt, mod[0:1], p, l, None)
        for acc_list, s in zip(outs, st):
            acc_list.append(s)
        ctx = {'k': cache_attn_k[:, l], 'v': cache_attn_v[:, l], 'ssd': state_ssd[:, l],
               'C': state_mlstm_C[:, l], 'n': state_mlstm_n[:, l], 'm': state_mlstm_m[:, l]}
        y_sample, _ = _trunk_layer(y_sample, mod[1:1 + nb], p, l, ctx)
    return (y_prompt, y_sample) + tuple(jnp.stack(o, axis=1) for o in outs)
```

```python
import functools
import math

import jax
import jax.numpy as jnp
import numpy as np
from jax import lax
from jax.experimental import pallas as pl
from jax.experimental.pallas import tpu as pltpu

F32 = jnp.float32
BF16 = jnp.bfloat16
HIGHEST = lax.Precision.HIGHEST

D_MODEL = 1024
DEPTH = 2
GRID_W = 64
HA = 4
DQK = 32
DVA = 2 * DQK
WA = HA * DVA
ROPE_BASE = 10000.0
HB = 8
PB = 64
WB = HB * PB
GB = 2
NB = 64
CONV_W = 3
CONV_CH = WB + 2 * GB * NB
HC = 4
DKC = 64
DVC = 64
WC = HC * DVC
D_MIX = WA + WB + WC
CHUNK = 128
N_EXPERTS = 32
TOP_K = 4
D_FF = D_MODEL
SWIGLU_LIMIT = 7.0
SWIGLU_ALPHA = 1.702
ALPHA = (2 * DEPTH) ** 0.25
LN_EPS = 1e-5

C_QA, C_KA, C_VA, C_OM = 0, 256, 512, 768
C_Z = 1024
C_XBC = 1536
C_QM, C_KM, C_VM = 2304, 2560, 2816
C_DTGM = 3072
N_PROJ = 3200
DT_LANES = 2 * HB
GM_LANE0 = DT_LANES

VMEM_LIMIT = 56 * 1024 * 1024
LANES = 128
SUBLANES = 8


def _cparams(*sem):
    return pltpu.CompilerParams(dimension_semantics=sem, vmem_limit_bytes=VMEM_LIMIT)


def _sigmoid(x):
    return 1.0 / (1.0 + jnp.exp(-x))


def _softplus(x):
    return jnp.maximum(x, 0.0) + jnp.log1p(jnp.exp(-jnp.abs(x)))


def _mod_kernel(c_ref, w_ref, b_ref, o_ref):
    c = c_ref[...]
    s = (c * _sigmoid(c)).astype(BF16)
    o_ref[...] = jnp.dot(s, w_ref[...].astype(BF16), preferred_element_type=F32) + b_ref[...]


def _modulation(cond, w_mod, b_mod):
    rows = cond.shape[0]
    n = w_mod.shape[1]
    tn = D_MODEL
    return pl.pallas_call(
        _mod_kernel,
        grid=(n // tn,),
        in_specs=[pl.BlockSpec((rows, D_MODEL), lambda j: (0, 0)),
                  pl.BlockSpec((D_MODEL, tn), lambda j: (0, j)),
                  pl.BlockSpec((1, tn), lambda j: (0, j))],
        out_specs=pl.BlockSpec((rows, tn), lambda j: (0, j)),
        out_shape=jax.ShapeDtypeStruct((rows, n), F32),
        compiler_params=_cparams("arbitrary"),
        name="modulation",
    )(cond, w_mod, b_mod.reshape(1, n))


def _value_blocks(v):
    lane = lax.broadcasted_iota(jnp.int32, (1, LANES), 1)
    out = []
    for h in range(HA):
        vt = v[:, (h // 2) * LANES:(h // 2 + 1) * LANES]
        if h % 2 == 1:
            vt = pltpu.roll(vt, DVA, 1)
        out.append(jnp.where(lane < DVA, vt, (lane == DVA).astype(F32)).astype(BF16))
    return out


def _inproj_kernel(x_ref, mod_ref, w_ref, cos_ref, sa_ref, sb_ref, *rest, rope):
    o_ref, kt_ref, va_ref = rest[-3:]
    x = x_ref[0]
    sh = mod_ref[0, 0:1, :]
    sc = mod_ref[0, 1:2, :]
    h = (x * (1.0 + sc) + sh).astype(BF16)
    p = jnp.dot(h, w_ref[...], preferred_element_type=F32)

    def rot(t):
        return (t * cos_ref[...] + pltpu.roll(t, WA - DQK // 4, 1) * sa_ref[...]
                + pltpu.roll(t, DQK // 4, 1) * sb_ref[...])

    q = p[:, C_QA:C_QA + WA]
    k = p[:, C_KA:C_KA + WA]
    if rope:
        q = rot(q)
        k = rot(k)
    o_ref[0, :, C_QA:C_QA + WA] = q * (DQK ** -0.5)
    o_ref[0, :, C_KA:C_KA + WA] = k
    o_ref[0, :, C_VA:] = p[:, C_VA:]
    kt_ref[0] = k.T.astype(BF16)
    for h, blk in enumerate(_value_blocks(p[:, C_VA:C_VA + WA])):
        va_ref[0, h] = blk


def _rope_tables(L):
    quarter = DQK // 4
    pos = np.arange(L)
    row = pos // GRID_W
    col = pos % GRID_W
    inv = ROPE_BASE ** (-np.arange(quarter, dtype=np.float32) / quarter)
    lane = np.arange(WA)
    c = lane % DQK
    use_col = (c // (DQK // 2)) == 1
    w = c % (DQK // 2)
    f = w % quarter
    first = w < quarter
    p = jnp.where(use_col[None, :], col[:, None], row[:, None]).astype(F32)
    ang = p * jnp.asarray(inv)[f][None, :]
    cos = jnp.cos(ang)
    sin = jnp.sin(ang)
    sa = jnp.where(first[None, :], -sin, 0.0)
    sb = jnp.where(first[None, :], 0.0, sin)
    return cos, sa, sb


def _inproj(x, mod, w_in_p, rope, ctx_bufs=None):
    B, L, _ = x.shape
    tl = min(L, 512)
    cos, sa, sb = _rope_tables(L)
    bm = mod.shape[0]
    mod_idx = (lambda b, i: (b, 0, 0)) if bm > 1 else (lambda b, i: (0, 0, 0))
    tab = pl.BlockSpec((tl, WA), lambda b, i: (i, 0))
    lk = L if ctx_bufs is None else ctx_bufs[0].shape[2]
    n_in = 6
    return pl.pallas_call(
        functools.partial(_inproj_kernel, rope=rope),
        grid=(B, L // tl),
        in_specs=[pl.BlockSpec((1, tl, D_MODEL), lambda b, i: (b, i, 0)),
                  pl.BlockSpec((1, 6, D_MODEL), mod_idx),
                  pl.BlockSpec((D_MODEL, N_PROJ), lambda b, i: (0, 0)),
                  tab, tab, tab] + ([] if ctx_bufs is None else [pl.BlockSpec(memory_space=pl.ANY)] * 2),
        out_specs=[pl.BlockSpec((1, tl, N_PROJ), lambda b, i: (b, i, 0)),
                   pl.BlockSpec((1, WA, tl), lambda b, i: (b, 0, i)),
                   pl.BlockSpec((1, HA, tl, LANES), lambda b, i: (b, 0, i, 0))],
        out_shape=[jax.ShapeDtypeStruct((B, L, N_PROJ), F32),
                   jax.ShapeDtypeStruct((B, WA, lk), BF16),
                   jax.ShapeDtypeStruct((B, HA, lk, LANES), BF16)],
        input_output_aliases={} if ctx_bufs is None else {n_in: 1, n_in + 1: 2},
        compiler_params=_cparams("arbitrary", "arbitrary"),
        name="inproj",
    )(x, mod, w_in_p, cos, sa, sb, *(() if ctx_bufs is None else ctx_bufs))


def _permute_w_in(w_in):
    sizes = (WA, WA, WA, WB, CONV_CH, 2 * HB, WC, WC, WC, WC, 4 * HC)
    offs = np.concatenate([[0], np.cumsum(sizes)])
    qa, ka, va, z, xbc, dt, qm, km, vm, om, gm = (w_in[:, offs[i]:offs[i + 1]] for i in range(11))
    pad = jnp.zeros((w_in.shape[0], N_PROJ - C_DTGM - 2 * HB - 4 * HC), w_in.dtype)
    return jnp.concatenate([qa, ka, va, om, z, xbc, qm, km, vm, dt, gm, pad], axis=1).astype(BF16)


def _attn_kernel(lamp_ref, q_ref, kt_ref, v_ref, g_ref, o_ref, *, lam_init):
    lp = lamp_ref[...]
    lam = (jnp.exp(jnp.sum(lp[0:1] * lp[1:2], axis=-1, keepdims=True))
           - jnp.exp(jnp.sum(lp[2:3] * lp[3:4], axis=-1, keepdims=True)) + lam_init)
    q = q_ref[0]
    outs = []
    for h in range(HA):
        os_ = []
        for m in range(2):
            c0 = (2 * h + m) * DQK
            s = jnp.dot(q[:, c0:c0 + DQK].astype(BF16), kt_ref[0, c0:c0 + DQK, :],
                        preferred_element_type=F32)
            p = jnp.exp(s - jnp.max(s, axis=-1, keepdims=True)).astype(BF16)
            os_.append(jnp.dot(p, v_ref[0, h], preferred_element_type=F32))
        seg = (os_[0][:, :DVA] * (1.0 / os_[0][:, DVA:DVA + 1])
               - os_[1][:, :DVA] * (lam / os_[1][:, DVA:DVA + 1]))
        ms = jnp.mean(seg * seg, axis=-1, keepdims=True)
        outs.append(seg * lax.rsqrt(ms + LN_EPS))
    o_ref[0] = jnp.concatenate(outs, axis=1) * g_ref[...] * (1.0 - lam_init)


def _attention(proj, kt, va, lam_params, attn_g, layer):
    B, L, _ = proj.shape
    lk = kt.shape[2]
    tq = min(L, 256)
    lam_init = 0.8 - 0.6 * math.exp(-0.3 * layer)
    return pl.pallas_call(
        functools.partial(_attn_kernel, lam_init=lam_init),
        grid=(B, L // tq),
        in_specs=[pl.BlockSpec((4, DQK), lambda b, i: (0, 0)),
                  pl.BlockSpec((1, tq, WA), lambda b, i: (b, i, C_QA // WA)),
                  pl.BlockSpec((1, WA, lk), lambda b, i: (b, 0, 0)),
                  pl.BlockSpec((1, HA, lk, LANES), lambda b, i: (b, 0, 0, 0)),
                  pl.BlockSpec((1, WA), lambda b, i: (0, 0))],
        out_specs=pl.BlockSpec((1, tq, WA), lambda b, i: (b, i, 0)),
        out_shape=jax.ShapeDtypeStruct((B, L, WA), F32),
        compiler_params=_cparams("arbitrary", "arbitrary"),
        name="diff_attention",
    )(lam_params, proj, kt, va, jnp.tile(attn_g, HA).reshape(1, WA))


def _ctx_attn_buffers(ctx_k, ctx_v, L):
    B, P = ctx_k.shape[:2]
    kt = jnp.swapaxes(ctx_k.reshape(B, P, WA), 1, 2).astype(BF16)
    vh = jnp.swapaxes(ctx_v, 1, 2).astype(BF16)
    va = jnp.concatenate([vh, jnp.ones((B, HA, P, 1), BF16), jnp.zeros((B, HA, P, LANES - DVA - 1), BF16)],
                         axis=-1)
    return (jnp.pad(kt, ((0, 0), (0, 0), (L, 0))), jnp.pad(va, ((0, 0), (0, 0), (L, 0), (0, 0))))


def _tri(rev):
    i = lax.broadcasted_iota(jnp.int32, (CHUNK, CHUNK), 0)
    j = lax.broadcasted_iota(jnp.int32, (CHUNK, CHUNK), 1)
    return (j >= i) if rev else (j <= i)


def _split3(x):
    hi = x.astype(BF16)
    r1 = x - hi.astype(F32)
    mid = r1.astype(BF16)
    lo = (r1 - mid.astype(F32)).astype(BF16)
    return hi, mid, lo


def _chunk_cumsum(a, rev):
    tri = _tri(rev).astype(BF16)
    return jnp.dot(jnp.concatenate([tri, tri, tri], axis=1), jnp.concatenate(_split3(a), axis=0),
                   preferred_element_type=F32)


def _widen(t, onehot3):
    return jnp.dot(jnp.concatenate(_split3(t), axis=1), onehot3, preferred_element_type=F32)


def _onehot3(src_lane_of_col, width):
    r = lax.broadcasted_iota(jnp.int32, (3 * LANES, width), 0) % LANES
    c = lax.broadcasted_iota(jnp.int32, (3 * LANES, width), 1)
    return (r == src_lane_of_col(c)).astype(BF16)


def _dir_chunk(c, nc, d):
    return c if d == 0 else nc - 1 - c


def _ssd_dir(x_ref, xp_ref, xn_ref, dt_ref, cw_ref, cb_ref, dtb_ref, alog_ref, ht, chunk, nc, d):
    rev = d == 1
    x = x_ref[0]
    rowid = lax.broadcasted_iota(jnp.int32, (CHUNK, 1), 0)
    prev = jnp.where(chunk == 0, 0.0, xp_ref[0, SUBLANES - 1:SUBLANES, :])
    nxt = jnp.where(chunk == nc - 1, 0.0, xn_ref[0, 0:1, :])
    xm1 = jnp.where(rowid == 0, prev, pltpu.roll(x, 1, 0))
    xp1 = jnp.where(rowid == CHUNK - 1, nxt, pltpu.roll(x, CHUNK - 1, 0))
    xc = cw_ref[0:1, :] * xm1 + cw_ref[1:2, :] * x + cw_ref[2:3, :] * xp1 + cb_ref[...]
    xc = xc * _sigmoid(xc)
    xs = xc[:, :WB]

    lane = lax.broadcasted_iota(jnp.int32, (1, LANES), 1)
    dt = _softplus(dt_ref[0] + dtb_ref[...])
    aneg = jnp.where(lane < DT_LANES, -jnp.exp(alog_ref[...]), 0.0)
    cum = _chunk_cumsum(dt * aneg, rev)
    cum_t = cum.T
    expand = _onehot3(lambda c: d * HB + c // PB, WB)
    dt_w = _widen(dt, expand)
    cum_w = _widen(cum, expand)
    tot_w = cum_w[0:1, :] if rev else cum_w[CHUNK - 1:CHUNK, :]
    e_w = jnp.exp(cum_w)
    te_w = jnp.exp(tot_w - cum_w)
    xin = xs * dt_w
    xw = (xin * te_w).astype(BF16)
    xin_b = xin.astype(BF16)
    mask = _tri(rev)
    ht_b = ht.astype(BF16)
    ys, sts = [], []
    gw = WB // GB
    for g in range(GB):
        bm = xc[:, WB + g * NB:WB + (g + 1) * NB].astype(BF16)
        cm = xc[:, WB + GB * NB + g * NB:WB + GB * NB + (g + 1) * NB].astype(BF16)
        cb = lax.dot_general(cm, bm, (((1,), (1,)), ((), ())), preferred_element_type=F32)
        yd = []
        for r in range(HB // GB):
            h = g * (HB // GB) + r
            hd = d * HB + h
            seg = cum[:, hd:hd + 1] - cum_t[hd:hd + 1, :]
            dec = jnp.exp(jnp.where(mask, seg, -jnp.inf))
            yd.append(jnp.dot((cb * dec).astype(BF16), xin_b[:, h * PB:(h + 1) * PB],
                              preferred_element_type=F32))
        y_off = jnp.dot(cm, ht_b[:, g * gw:(g + 1) * gw], preferred_element_type=F32)
        ys.append(jnp.concatenate(yd, axis=1) + y_off * e_w[:, g * gw:(g + 1) * gw])
        sts.append(lax.dot_general(bm, xw[:, g * gw:(g + 1) * gw], (((0,), (0,)), ((), ())),
                                   preferred_element_type=F32))
    y = jnp.concatenate(ys, axis=1)
    cd = e_w[0:1, :] if rev else e_w[CHUNK - 1:CHUNK, :]
    ht_new = ht * cd + jnp.concatenate(sts, axis=1)
    return y, xs, ht_new


def _ssd_kernel(xf_ref, xfp_ref, xfn_ref, dtf_ref, xb_ref, xbp_ref, xbn_ref, dtb_ref,
                cw_ref, cb_ref, dtbias_ref, alog_ref, dskip_ref, h0_ref,
                yf_ref, yb_ref, hfin_ref, h_scr, *, nc):
    c = pl.program_id(1)

    @pl.when(c == 0)
    def _():
        h_scr[...] = h0_ref[0]

    yf, xs_f, hf = _ssd_dir(xf_ref, xfp_ref, xfn_ref, dtf_ref, cw_ref, cb_ref, dtbias_ref, alog_ref,
                            h_scr[0], c, nc, 0)
    yb, _, hb = _ssd_dir(xb_ref, xbp_ref, xbn_ref, dtb_ref, cw_ref, cb_ref, dtbias_ref, alog_ref,
                         h_scr[1], nc - 1 - c, nc, 1)
    yf_ref[0] = yf + xs_f * dskip_ref[...]
    yb_ref[0] = yb
    h_scr[0] = hf
    h_scr[1] = hb

    @pl.when(c == nc - 1)
    def _():
        hfin_ref[0] = h_scr[...]


def _pad_row(v, width=LANES):
    v = v.reshape(1, -1).astype(F32)
    return jnp.pad(v, ((0, 0), (0, width - v.shape[1])))


def _ssd(proj, conv_w, conv_b, dt_bias, a_log, d_skip, h0):
    B, L, _ = proj.shape
    nc = L // CHUNK
    r8 = CHUNK // SUBLANES
    nb8 = L // SUBLANES
    xblk = C_XBC // CONV_CH
    dblk = C_DTGM // LANES

    def specs(d):
        ch = lambda c: _dir_chunk(c, nc, d)
        return [
            pl.BlockSpec((1, CHUNK, CONV_CH), lambda b, c: (b, ch(c), xblk)),
            pl.BlockSpec((1, SUBLANES, CONV_CH), lambda b, c: (b, jnp.maximum(ch(c) * r8 - 1, 0), xblk)),
            pl.BlockSpec((1, SUBLANES, CONV_CH), lambda b, c: (b, jnp.minimum((ch(c) + 1) * r8, nb8 - 1), xblk)),
            pl.BlockSpec((1, CHUNK, LANES), lambda b, c: (b, ch(c), dblk)),
        ]

    const = lambda shape: pl.BlockSpec(shape, lambda b, c: (0,) * len(shape))
    state = pl.BlockSpec((1, 2, NB, WB), lambda b, c: (b, 0, 0, 0))
    return pl.pallas_call(
        functools.partial(_ssd_kernel, nc=nc),
        grid=(B, nc),
        in_specs=specs(0) + specs(1) + [const((CONV_W, CONV_CH)), const((1, CONV_CH)), const((1, LANES)),
                                        const((1, LANES)), const((1, WB)), state],
        out_specs=[pl.BlockSpec((1, CHUNK, WB), lambda b, c: (b, c, 0)),
                   pl.BlockSpec((1, CHUNK, WB), lambda b, c: (b, nc - 1 - c, 0)),
                   state],
        out_shape=[jax.ShapeDtypeStruct((B, L, WB), F32), jax.ShapeDtypeStruct((B, L, WB), F32),
                   jax.ShapeDtypeStruct((B, 2, NB, WB), F32)],
        scratch_shapes=[pltpu.VMEM((2, NB, WB), F32)],
        compiler_params=_cparams("arbitrary", "arbitrary"),
        name="ssd_scan",
    )(proj, proj, proj, proj, proj, proj, proj, proj,
      conv_w, conv_b.reshape(1, CONV_CH), _pad_row(dt_bias), _pad_row(a_log),
      jnp.repeat(d_skip, PB).reshape(1, WB), h0)


def _ssd_state_in(state):
    B = state.shape[0]
    return jnp.transpose(state, (0, 1, 4, 2, 3)).reshape(B, 2, NB, WB)


def _ssd_state_out(ht):
    B = ht.shape[0]
    return jnp.transpose(ht.reshape(B, 2, NB, HB, PB), (0, 1, 3, 4, 2))


def _fg_lane(d, h):
    return GM_LANE0 + d * 2 * HC + HC + h


def _mlstm_dir(q_ref, k_ref, v_ref, g_ref, gbias_ref, cbd, nmat, m_prev, d):
    rev = d == 1
    q_b = q_ref[0].astype(BF16)
    k = k_ref[0] * (DKC ** -0.5)
    k_b = k.astype(BF16)
    v_b = v_ref[0].astype(BF16)
    pre = g_ref[0] + gbias_ref[...]
    bc = _chunk_cumsum(-_softplus(-pre), rev)
    ig = pltpu.roll(pre, HC, 1)
    tot = bc[0:1, :] if rev else bc[CHUNK - 1:CHUNK, :]
    gend = tot - bc + ig
    mloc = jnp.max(gend, axis=0, keepdims=True)
    wend = jnp.exp(gend - mloc)
    m_new = jnp.maximum(tot + m_prev, mloc)
    a_prev = jnp.exp(tot + m_prev - m_new)
    a_loc = jnp.exp(mloc - m_new)
    inter = bc + m_prev
    bc_t = bc.T
    ig_t = ig.T
    lf0 = _fg_lane(d, 0)
    to_tile = _onehot3(lambda c: lf0 + c // LANES, HC * LANES)
    to_head = _onehot3(lambda c: lf0 + c // DVC, WC)
    lane = lax.broadcasted_iota(jnp.int32, (1, LANES), 1)
    mask = _tri(rev)
    bc_w = _widen(bc, to_tile)
    dms = []
    rmax = jnp.zeros((CHUNK, LANES), F32)
    for h in range(HC):
        lf = lf0 + h
        dm = jnp.where(mask, bc_w[:, h * LANES:(h + 1) * LANES] - bc_t[lf:lf + 1, :] + ig_t[lf:lf + 1, :],
                       -jnp.inf)
        dms.append(dm)
        rmax = jnp.where(lane == lf, jnp.max(dm, axis=-1, keepdims=True), rmax)
    mt = jnp.maximum(inter, rmax)
    mt_w = _widen(mt, to_tile)
    w_inter = jnp.exp(inter - mt)
    nums = []
    den_in = jnp.zeros((CHUNK, LANES), F32)
    for h in range(HC):
        hk = slice(h * DKC, (h + 1) * DKC)
        qk = lax.dot_general(q_b[:, hk], k_b[:, hk], (((1,), (1,)), ((), ())), preferred_element_type=F32)
        qk = qk * jnp.exp(dms[h] - mt_w[:, h * LANES:(h + 1) * LANES])
        nums.append(jnp.dot(qk.astype(BF16), v_b[:, h * DVC:(h + 1) * DVC], preferred_element_type=F32))
        den_in = jnp.where(lane == lf0 + h, jnp.sum(qk, axis=-1, keepdims=True), den_in)
    cross_num = jnp.dot(q_b, cbd.astype(BF16), preferred_element_type=F32)
    cross_den = jnp.dot(q_b, nmat.astype(BF16), preferred_element_type=F32)
    den = den_in + w_inter * cross_den
    head_lane = (lane >= lf0) & (lane < lf0 + HC)
    rinv = jnp.where(head_lane, 1.0 / jnp.maximum(jnp.abs(den), jnp.exp(-mt)), 0.0)
    h_out = (jnp.concatenate(nums, axis=1) + _widen(w_inter, to_head) * cross_num) * _widen(rinv, to_head)
    kw = (k * _widen(wend, to_head)).astype(BF16)
    tn = (((0,), (0,)), ((), ()))
    rr = lax.broadcasted_iota(jnp.int32, (WC, WC), 0) // DKC
    cc = lax.broadcasted_iota(jnp.int32, (WC, WC), 1) // DVC
    c_loc = jnp.where(rr == cc, lax.dot_general(kw, v_b, tn, preferred_element_type=F32), 0.0)
    nr = lax.broadcasted_iota(jnp.int32, (WC, LANES), 0) // DKC
    nl = lax.broadcasted_iota(jnp.int32, (WC, LANES), 1)
    n_loc = jnp.where(nl == lf0 + nr,
                      lax.dot_general(kw, jnp.ones((CHUNK, LANES), BF16), tn, preferred_element_type=F32), 0.0)
    cbd_new = cbd * _widen(a_prev, to_head) + c_loc * _widen(a_loc, to_head)
    nmat_new = nmat * a_prev + n_loc * a_loc
    return h_out, cbd_new, nmat_new, m_new


def _mlstm_kernel(qf_ref, kf_ref, vf_ref, gf_ref, qb_ref, kb_ref, vb_ref, gb_ref, gbias_ref,
                  c0_ref, n0_ref, m0_ref, hf_ref, hb_ref, cfin_ref, nfin_ref, mfin_ref,
                  c_scr, n_scr, m_scr, *, nc):
    c = pl.program_id(1)

    @pl.when(c == 0)
    def _():
        c_scr[...] = c0_ref[0]
        n_scr[...] = n0_ref[0]
        m_scr[...] = m0_ref[0]

    hf, cf, nf, mf = _mlstm_dir(qf_ref, kf_ref, vf_ref, gf_ref, gbias_ref, c_scr[0], n_scr[0], m_scr[0], 0)
    hb, cb, nb_, mb = _mlstm_dir(qb_ref, kb_ref, vb_ref, gb_ref, gbias_ref, c_scr[1], n_scr[1], m_scr[1], 1)
    hf_ref[0] = hf
    hb_ref[0] = hb
    c_scr[0] = cf
    c_scr[1] = cb
    n_scr[0] = nf
    n_scr[1] = nb_
    m_scr[0] = mf
    m_scr[1] = mb

    @pl.when(c == nc - 1)
    def _():
        cfin_ref[0] = c_scr[...]
        nfin_ref[0] = n_scr[...]
        mfin_ref[0] = m_scr[...]


def _mlstm(proj, ig_b, fg_b, c0, n0, m0):
    B, L, _ = proj.shape
    nc = L // CHUNK
    gbias = jnp.concatenate([ig_b[0], fg_b[0], ig_b[1], fg_b[1]]).astype(F32)
    gbias = jnp.pad(gbias, (GM_LANE0, LANES - GM_LANE0 - 4 * HC)).reshape(1, LANES)

    def specs(d):
        ch = lambda c: _dir_chunk(c, nc, d)
        col = lambda off: pl.BlockSpec((1, CHUNK, WC), lambda b, c: (b, ch(c), off // WC))
        return [col(C_QM), col(C_KM), col(C_VM),
                pl.BlockSpec((1, CHUNK, LANES), lambda b, c: (b, ch(c), C_DTGM // LANES))]

    cstate = pl.BlockSpec((1, 2, WC, WC), lambda b, c: (b, 0, 0, 0))
    nstate = pl.BlockSpec((1, 2, WC, LANES), lambda b, c: (b, 0, 0, 0))
    mstate = pl.BlockSpec((1, 2, 1, LANES), lambda b, c: (b, 0, 0, 0))
    return pl.pallas_call(
        functools.partial(_mlstm_kernel, nc=nc),
        grid=(B, nc),
        in_specs=specs(0) + specs(1) + [pl.BlockSpec((1, LANES), lambda b, c: (0, 0)), cstate, nstate, mstate],
        out_specs=[pl.BlockSpec((1, CHUNK, WC), lambda b, c: (b, c, 0)),
                   pl.BlockSpec((1, CHUNK, WC), lambda b, c: (b, nc - 1 - c, 0)),
                   cstate, nstate, mstate],
        out_shape=[jax.ShapeDtypeStruct((B, L, WC), F32), jax.ShapeDtypeStruct((B, L, WC), F32),
                   jax.ShapeDtypeStruct((B, 2, WC, WC), F32),
                   jax.ShapeDtypeStruct((B, 2, WC, LANES), F32),
                   jax.ShapeDtypeStruct((B, 2, 1, LANES), F32)],
        scratch_shapes=[pltpu.VMEM((2, WC, WC), F32), pltpu.VMEM((2, WC, LANES), F32),
                        pltpu.VMEM((2, 1, LANES), F32)],
        compiler_params=_cparams("arbitrary", "arbitrary"),
        name="mlstm_scan",
    )(proj, proj, proj, proj, proj, proj, proj, proj, gbias, c0, n0, m0)


def _mlstm_state_in(C, n, m):
    if isinstance(C, int):
        return (jnp.zeros((C, 2, WC, WC), F32), jnp.zeros((C, 2, WC, LANES), F32),
                jnp.zeros((C, 2, 1, LANES), F32))
    B = C.shape[0]
    eye = jnp.eye(HC, dtype=F32)
    c0 = jnp.einsum('bdhkv,hg->bdhkgv', C, eye).reshape(B, 2, WC, WC)
    lanes = jnp.stack([jnp.arange(HC) + _fg_lane(d, 0) for d in range(2)])
    onehot = (lanes[:, :, None] == jnp.arange(LANES)[None, None, :]).astype(F32)
    n0 = jnp.einsum('bdhk,dhl->bdhkl', n, onehot).reshape(B, 2, WC, LANES)
    m0 = jnp.einsum('bdh,dhl->bdl', m, onehot).reshape(B, 2, 1, LANES)
    return c0, n0, m0


def _mlstm_state_out(cfin, nfin, mfin):
    B = cfin.shape[0]
    c5 = cfin.reshape(B, 2, HC, DKC, HC, DVC)
    C = jnp.stack([c5[:, :, h, :, h, :] for h in range(HC)], axis=2)
    n4 = nfin.reshape(B, 2, HC, DKC, LANES)
    n = jnp.stack([jnp.stack([n4[:, d, h, :, _fg_lane(d, h)] for h in range(HC)], axis=1) for d in range(2)], axis=1)
    m = jnp.stack([mfin[:, d, 0, _fg_lane(d, 0):_fg_lane(d, 0) + HC] for d in range(2)], axis=1)
    return C, n, m


RT_IDX0 = 0
RT_GATE0 = TOP_K
RT_ROWS = 2 * TOP_K


def _layer_norm_rows(x, g, b):
    mu = jnp.mean(x, axis=-1, keepdims=True)
    xc = x - mu
    var = jnp.mean(xc * xc, axis=-1, keepdims=True)
    return xc * lax.rsqrt(var + LN_EPS) * g + b


def _outproj_kernel(oa_ref, yf_ref, yb_ref, z_ref, hf_ref, hb_ref, om_ref, x_ref, mod_ref, w_ref,
                    sg_ref, mg_ref, lg_ref, lb_ref, rw_ref, rb_ref, x1_ref, h2_ref, rt_ref):
    z = z_ref[0]
    y = (yf_ref[0] + yb_ref[0]) * (z * _sigmoid(z))
    gw = WB // GB
    ob = []
    for g in range(GB):
        seg = y[:, g * gw:(g + 1) * gw]
        ob.append(seg * lax.rsqrt(jnp.mean(seg * seg, axis=-1, keepdims=True) + LN_EPS))
    ob = jnp.concatenate(ob, axis=1) * sg_ref[...]
    hsum = hf_ref[0] + hb_ref[0]
    oc = []
    for h in range(HC):
        seg = hsum[:, h * DVC:(h + 1) * DVC]
        mu = jnp.mean(seg, axis=-1, keepdims=True)
        sc = seg - mu
        oc.append(sc * lax.rsqrt(jnp.mean(sc * sc, axis=-1, keepdims=True) + LN_EPS))
    oc = jnp.concatenate(oc, axis=1) * mg_ref[...] * _sigmoid(om_ref[0])
    mix_in = jnp.concatenate([oa_ref[0], ob, oc], axis=1).astype(BF16)
    mix = jnp.dot(mix_in, w_ref[...], preferred_element_type=F32)
    x1 = _layer_norm_rows(ALPHA * x_ref[0] + mod_ref[0, 2:3, :] * mix, lg_ref[...], lb_ref[...])
    x1_ref[0] = x1
    h2 = x1 * (1.0 + mod_ref[0, 4:5, :]) + mod_ref[0, 3:4, :]
    for j in range(ROW_CH):
        h2_ref[pl.ds(j, h2.shape[0], stride=ROW_CH), :] = h2[:, j * LANES:(j + 1) * LANES]
    h_hi = h2.astype(BF16)
    h_mid = (h2 - h_hi.astype(F32)).astype(BF16)
    logits = jnp.dot(jnp.concatenate([h_hi, h_hi, h_mid], axis=1), rw_ref[...],
                     preferred_element_type=F32) + rb_ref[...]
    lane = lax.broadcasted_iota(jnp.int32, (1, LANES), 1)
    lane_f = lane.astype(F32)
    lg = jnp.where(lane < N_EXPERTS, logits, -jnp.inf)
    vals, ids = [], []
    for _ in range(TOP_K):
        mx = jnp.max(lg, axis=-1, keepdims=True)
        first = jnp.min(jnp.where(lg == mx, lane_f, float(LANES)), axis=-1, keepdims=True)
        vals.append(mx)
        ids.append(first)
        lg = jnp.where(lane_f == first, -jnp.inf, lg)
    es = [jnp.exp(v - vals[0]) for v in vals]
    inv = 1.0 / (es[0] + es[1] + es[2] + es[3])
    rt = jnp.zeros(logits.shape, F32)
    for k in range(TOP_K):
        rt = jnp.where(lane == RT_IDX0 + k, ids[k], rt)
        rt = jnp.where(lane == RT_GATE0 + k, es[k] * inv, rt)
    rt_ref[...] = rt.T[0:RT_ROWS, :]


def _outproj(oa, yf, yb, hf, hb, proj, x, mod, w_out_b, ssd_g, mlstm_g, ln_g, ln_b, router_w, router_b):
    B, L, _ = x.shape
    tl = min(L, 256)
    bm = mod.shape[0]
    mod_idx = (lambda b, i: (b, 0, 0)) if bm > 1 else (lambda b, i: (0, 0, 0))
    row = lambda w: pl.BlockSpec((1, tl, w), lambda b, i: (b, i, 0))
    const = lambda shape: pl.BlockSpec(shape, lambda b, i: (0,) * len(shape))
    rw = jnp.pad(router_w.astype(F32), ((0, 0), (0, LANES - N_EXPERTS)))
    rw_hi = rw.astype(BF16)
    rw_mid = (rw - rw_hi.astype(F32)).astype(BF16)
    rw = jnp.concatenate([rw_hi, rw_mid, rw_hi], axis=0)
    return pl.pallas_call(
        _outproj_kernel,
        grid=(B, L // tl),
        in_specs=[row(WA), row(WB), row(WB),
                  pl.BlockSpec((1, tl, WB), lambda b, i: (b, i, C_Z // WB)),
                  row(WC), row(WC),
                  pl.BlockSpec((1, tl, WC), lambda b, i: (b, i, C_OM // WC)),
                  row(D_MODEL),
                  pl.BlockSpec((1, 6, D_MODEL), mod_idx),
                  const((D_MIX, D_MODEL)), const((1, WB)), const((1, WC)),
                  const((1, D_MODEL)), const((1, D_MODEL)), const((3 * D_MODEL, LANES)), const((1, LANES))],
        out_specs=[row(D_MODEL),
                   pl.BlockSpec((tl * ROW_CH, LANES), lambda b, i: (b * (L // tl) + i, 0)),
                   pl.BlockSpec((RT_ROWS, tl), lambda b, i: (0, b * (L // tl) + i))],
        out_shape=[jax.ShapeDtypeStruct((B, L, D_MODEL), F32),
                   jax.ShapeDtypeStruct((B * L * ROW_CH, LANES), F32),
                   jax.ShapeDtypeStruct((RT_ROWS, B * L), F32)],
        compiler_params=_cparams("arbitrary", "arbitrary"),
        name="outproj_ln_route",
    )(oa, yf, yb, proj, hf, hb, proj, x, mod, w_out_b, ssd_g.reshape(1, WB), mlstm_g.reshape(1, WC),
      ln_g.reshape(1, D_MODEL), ln_b.reshape(1, D_MODEL), rw, _pad_row(router_b))


def _final_ln_kernel(x_ref, ff_ref, mod_ref, g_ref, b_ref, o_ref):
    tl = x_ref.shape[1]
    ff = jnp.concatenate([ff_ref[pl.ds(j, tl, stride=ROW_CH), :] for j in range(ROW_CH)], axis=1)
    o_ref[0] = _layer_norm_rows(ALPHA * x_ref[0] + mod_ref[0, 5:6, :] * ff, g_ref[...], b_ref[...])


def _final_ln(x1, ff, mod, ln_g, ln_b):
    B, L, _ = x1.shape
    tl = min(L, 512)
    bm = mod.shape[0]
    mod_idx = (lambda b, i: (b, 0, 0)) if bm > 1 else (lambda b, i: (0, 0, 0))
    row = pl.BlockSpec((1, tl, D_MODEL), lambda b, i: (b, i, 0))
    const = pl.BlockSpec((1, D_MODEL), lambda b, i: (0, 0))
    return pl.pallas_call(
        _final_ln_kernel,
        grid=(B, L // tl),
        in_specs=[row, pl.BlockSpec((tl * ROW_CH, LANES), lambda b, i: (b * (L // tl) + i, 0)),
                  pl.BlockSpec((1, 6, D_MODEL), mod_idx), const, const],
        out_specs=row,
        out_shape=jax.ShapeDtypeStruct((B, L, D_MODEL), F32),
        compiler_params=_cparams("arbitrary", "arbitrary"),
        name="final_ln",
    )(x1, ff, mod, ln_g.reshape(1, D_MODEL), ln_b.reshape(1, D_MODEL))


MOE_TM = 2048
MOE_R = 256
MOE_U = SUBLANES
MOE_S = MOE_R + 1
MOE_RB = 512
ROW_CH = D_MODEL // LANES
MOE_LMAX = MOE_TM * TOP_K + N_EXPERTS * MOE_U


def _ceil_to_unit(x):
    return jnp.floor((x + (MOE_U - 1.0)) * (1.0 / MOE_U)) * float(MOE_U)


def _build_routing(rt_ref, d_vm, d_sm, g_sm, idx_sm, gate_sm, sems):
    r = rt_ref[...]
    eio = lax.broadcasted_iota(jnp.int32, (LANES, MOE_TM), 0).astype(F32)
    hit = [r[k:k + 1, :] == eio for k in range(TOP_K)]
    sel = jnp.zeros((LANES, MOE_TM), F32)
    for k in range(TOP_K):
        sel = jnp.where(hit[k], 1.0, sel)
    sel_b = sel.astype(BF16)
    tr = lax.broadcasted_iota(jnp.int32, (MOE_RB, MOE_RB), 0)
    tc = lax.broadcasted_iota(jnp.int32, (MOE_RB, MOE_RB), 1)
    before = (tr < tc).astype(BF16)
    run = jnp.zeros((LANES, 1), F32)
    ranks = []
    for c in range(MOE_TM // MOE_RB):
        blk = slice(c * MOE_RB, (c + 1) * MOE_RB)
        ranks.append(jnp.dot(sel_b[:, blk], before, preferred_element_type=F32) + run)
        run = run + jnp.sum(sel[:, blk], axis=1, keepdims=True)
    rank = jnp.concatenate(ranks, axis=1)
    pc_col = _ceil_to_unit(run)
    er = lax.broadcasted_iota(jnp.int32, (LANES, LANES), 0)
    ec = lax.broadcasted_iota(jnp.int32, (LANES, LANES), 1)
    offs_col = jnp.dot((ec < er).astype(F32), jnp.broadcast_to(pc_col, (LANES, LANES)),
                       precision=HIGHEST, preferred_element_type=F32)[:, 0:1]
    cnt_row = lax.dot_general(jnp.ones((SUBLANES, MOE_TM), BF16), sel_b, (((1,), (1,)), ((), ())),
                              preferred_element_type=F32)
    offs_row = jnp.dot(_ceil_to_unit(cnt_row), (er < ec).astype(F32), precision=HIGHEST,
                       preferred_element_type=F32)
    dest = offs_col + rank
    dest4 = [jnp.sum(jnp.where(hit[k], dest, 0.0), axis=0, keepdims=True) for k in range(TOP_K)]
    tail = jnp.concatenate([offs_row[0:RT_ROWS - TOP_K, :],
                            jnp.zeros((RT_ROWS - TOP_K, MOE_TM - LANES), F32)], axis=1)
    d_vm[...] = jnp.concatenate(dest4 + [tail], axis=0).astype(jnp.int32)
    cp_d = pltpu.make_async_copy(d_vm, d_sm, sems.at[0])
    cp_g = pltpu.make_async_copy(rt_ref, g_sm, sems.at[1])
    cp_d.start()
    cp_g.start()
    cp_d.wait()
    cp_g.wait()

    def pad_body(ex, c):
        end = d_sm[TOP_K, ex + 1]
        for u in range(MOE_U):
            pos = jnp.maximum(end - 1 - u, 0)
            idx_sm[pos] = MOE_TM
            gate_sm[pos] = 0.0
        return c

    lax.fori_loop(0, N_EXPERTS, pad_body, 0)

    def place_body(tb, c):
        t0 = pl.multiple_of(tb * LANES, LANES)
        for u in range(LANES):
            for k in range(TOP_K):
                pos = d_sm[k, t0 + u]
                idx_sm[pos] = t0 + u
                gate_sm[pos] = g_sm[RT_GATE0 + k, t0 + u]
        return c

    lax.fori_loop(0, MOE_TM // LANES, place_body, 0)


def _moe_kernel(rt_ref, x_ref, wgu_ref, bgu_ref, wdn_ref, bdn_ref, o_hbm,
                acc, xt, yt, d_vm, d_sm, g_sm, idx_sm, gate_sm, pend_sm, sems):
    i = pl.program_id(0)
    e = pl.program_id(1)

    def gather_pass(start):
        for mi in range(MOE_R):
            t = jnp.minimum(idx_sm[start + mi], MOE_TM - 1)
            src = pl.multiple_of(t * ROW_CH, ROW_CH)
            xt[pl.ds(mi, ROW_CH, stride=MOE_S), :] = x_ref[pl.ds(src, ROW_CH), :]

    def scatter_pass(start, n):
        for gi in range(MOE_R // MOE_U):
            valid = gi * MOE_U < n
            upd = []
            for u in range(MOE_U):
                mi = gi * MOE_U + u
                t = jnp.where(valid, idx_sm[start + mi], MOE_TM)
                gt = jnp.where(valid, gate_sm[start + mi], 0.0)
                dst = pl.multiple_of(t * ROW_CH, ROW_CH)
                upd.append((dst, acc[pl.ds(dst, ROW_CH), :] + gt * yt[pl.ds(mi, ROW_CH, stride=MOE_S), :]))
            for dst, val in upd:
                acc[pl.ds(dst, ROW_CH), :] = val

    @pl.when((i == 0) & (e == 0))
    def _():
        xt[...] = jnp.zeros(xt.shape, F32)
        yt[...] = jnp.zeros(yt.shape, F32)

        def clear(j, c):
            idx_sm[j] = 0
            gate_sm[j] = 0.0
            return c

        lax.fori_loop(0, MOE_LMAX + MOE_R, clear, 0)

    @pl.when(e == 0)
    def _():
        acc[...] = jnp.zeros(acc.shape, F32)
        _build_routing(rt_ref, d_vm, d_sm, g_sm, idx_sm, gate_sm, sems)
        pend_sm[0] = 0
        pend_sm[1] = 0
        gather_pass(0)

    base = d_sm[TOP_K, e]
    end = d_sm[TOP_K, e + 1]
    npass = lax.shift_right_logical(end - base + (MOE_R - 1), MOE_R.bit_length() - 1)

    def pass_body(c, carry):
        r0 = base + c * MOE_R
        x = jnp.concatenate([xt[pl.ds(j * MOE_S, MOE_R), :] for j in range(ROW_CH)], axis=1).astype(BF16)
        scatter_pass(pend_sm[0], pend_sm[1])
        gather_pass(jnp.minimum(r0 + MOE_R, end))
        gu = jnp.dot(x, wgu_ref[0], preferred_element_type=F32) + bgu_ref[0]
        g = jnp.minimum(gu[:, :D_FF], SWIGLU_LIMIT)
        u_ = jnp.clip(gu[:, D_FF:], -SWIGLU_LIMIT, SWIGLU_LIMIT)
        act = g * _sigmoid(SWIGLU_ALPHA * g) * (u_ + 1.0)
        y = jnp.dot(act.astype(BF16), wdn_ref[0], preferred_element_type=F32) + bdn_ref[0]
        for j in range(ROW_CH):
            yt[pl.ds(j * MOE_S, MOE_R), :] = y[:, j * LANES:(j + 1) * LANES]
        pend_sm[0] = r0
        pend_sm[1] = jnp.minimum(MOE_R, end - r0)
        return carry

    lax.fori_loop(0, npass, pass_body, 0)

    @pl.when(e == N_EXPERTS - 1)
    def _():
        scatter_pass(pend_sm[0], pend_sm[1])
        rows = MOE_TM * ROW_CH
        cp = pltpu.make_async_copy(acc.at[pl.ds(0, rows), :],
                                   o_hbm.at[pl.ds(pl.multiple_of(i * rows, rows), rows), :], sems.at[2])
        cp.start()
        cp.wait()


def _moe(xv, rt, w_gu_b, b_gu, w_dn_b, b_dn):
    T = xv.shape[0] // ROW_CH
    nt = T // MOE_TM
    return pl.pallas_call(
        _moe_kernel,
        grid=(nt, N_EXPERTS),
        in_specs=[pl.BlockSpec((RT_ROWS, MOE_TM), lambda i, e: (0, i)),
                  pl.BlockSpec((MOE_TM * ROW_CH, LANES), lambda i, e: (i, 0)),
                  pl.BlockSpec((1, D_MODEL, 2 * D_FF), lambda i, e: (e, 0, 0)),
                  pl.BlockSpec((1, 1, 2 * D_FF), lambda i, e: (e, 0, 0)),
                  pl.BlockSpec((1, D_FF, D_MODEL), lambda i, e: (e, 0, 0)),
                  pl.BlockSpec((1, 1, D_MODEL), lambda i, e: (e, 0, 0))],
        out_specs=pl.BlockSpec(memory_space=pl.ANY),
        scratch_shapes=[pltpu.VMEM(((MOE_TM + 1) * ROW_CH, LANES), F32),
                        pltpu.VMEM((ROW_CH * MOE_S, LANES), F32),
                        pltpu.VMEM((ROW_CH * MOE_S, LANES), F32),
                        pltpu.VMEM((RT_ROWS, MOE_TM), jnp.int32),
                        pltpu.SMEM((RT_ROWS, MOE_TM), jnp.int32),
                        pltpu.SMEM((RT_ROWS, MOE_TM), F32),
                        pltpu.SMEM((MOE_LMAX + MOE_R,), jnp.int32),
                        pltpu.SMEM((MOE_LMAX + MOE_R,), F32),
                        pltpu.SMEM((2,), jnp.int32),
                        pltpu.SemaphoreType.DMA((3,))],
        out_shape=jax.ShapeDtypeStruct((T * ROW_CH, LANES), F32),
        compiler_params=_cparams("arbitrary", "arbitrary"),
        name="moe_ffn",
    )(rt, xv, w_gu_b, b_gu.reshape(N_EXPERTS, 1, 2 * D_FF), w_dn_b, b_dn.reshape(N_EXPERTS, 1, D_MODEL))


def _trunk_layer(x, mod, p, layer, ctx):
    B, L, _ = x.shape
    ctx_bufs = None if ctx is None else _ctx_attn_buffers(ctx['k'], ctx['v'], L)
    proj, kt, va = _inproj(x, mod, p['w_in'], rope=ctx is not None, ctx_bufs=ctx_bufs)
    oa = _attention(proj, kt, va, p['lam'], p['attn_g'], layer)
    if ctx is None:
        h0 = jnp.zeros((B, 2, NB, WB), F32)
        c0, n0, m0 = _mlstm_state_in(B, None, None)
    else:
        h0 = _ssd_state_in(ctx['ssd'])
        c0, n0, m0 = _mlstm_state_in(ctx['C'], ctx['n'], ctx['m'])
    yf, yb, hfin = _ssd(proj, p['conv_w'], p['conv_b'], p['dt_bias'], p['a_log'], p['d_skip'], h0)
    hf, hb, cfin, nfin, mfin = _mlstm(proj, p['ig_b'], p['fg_b'], c0, n0, m0)
    x1, h2, rt = _outproj(oa, yf, yb, hf, hb, proj, x, mod, p['w_out'], p['ssd_g'], p['mlstm_g'],
                          p['ln1_g'], p['ln1_b'], p['router_w'], p['router_b'])
    ff = _moe(h2, rt, p['w_gu'], p['b_gu'], p['w_dn'], p['b_dn'])
    x2 = _final_ln(x1, ff, mod, p['ln2_g'], p['ln2_b'])
    new_ctx = None
    if ctx is None:
        c_out, n_out, m_out = _mlstm_state_out(cfin, nfin, mfin)
        new_ctx = (proj[:, :, C_KA:C_KA + WA].reshape(B, L, HA, 2 * DQK),
                   proj[:, :, C_VA:C_VA + WA].reshape(B, L, HA, DVA),
                   _ssd_state_out(hfin), c_out, n_out, m_out)
    return x2, new_ctx


def kernel(x_prompt, x_sample, c, cache_attn_k, cache_attn_v, state_ssd, state_mlstm_C, state_mlstm_n, state_mlstm_m, c_ctx, w_mod, b_mod, w_in, lam_q1, lam_k1, lam_q2, lam_k2, attn_g, conv_w, conv_b, dt_bias, a_log, d_skip, ssd_g, ig_b, fg_b, mlstm_g, w_out, ln1_g, ln1_b, router_w, router_b, w_gu, b_gu, w_dn, b_dn, ln2_g, ln2_b):
    nb = c.shape[0]
    cond = jnp.concatenate([c_ctx[None, :], c, jnp.zeros((2 * SUBLANES - 1 - nb, D_MODEL), F32)], axis=0)
    y_prompt, y_sample = x_prompt, x_sample
    outs = [[] for _ in range(6)]
    for l in range(DEPTH):
        p = {'w_in': _permute_w_in(w_in[l]),
             'lam': jnp.stack([lam_q1[l], lam_k1[l], lam_q2[l], lam_k2[l]]),
             'attn_g': attn_g[l], 'conv_w': conv_w[l], 'conv_b': conv_b[l], 'dt_bias': dt_bias[l],
             'a_log': a_log[l], 'd_skip': d_skip[l], 'ssd_g': ssd_g[l], 'ig_b': ig_b[l], 'fg_b': fg_b[l],
             'mlstm_g': mlstm_g[l], 'w_out': w_out[l].astype(BF16), 'ln1_g': ln1_g[l], 'ln1_b': ln1_b[l],
             'router_w': router_w[l], 'router_b': router_b[l], 'w_gu': w_gu[l].astype(BF16), 'b_gu': b_gu[l],
             'w_dn': w_dn[l].astype(BF16), 'b_dn': b_dn[l], 'ln2_g': ln2_g[l], 'ln2_b': ln2_b[l]}
        mod = _modulation(cond, w_mod[l], b_mod[l]).reshape(2 * SUBLANES, 6, D_MODEL)
        y_prompt, st = _trunk_layer(y_prompt, mod[0:1], p, l, None)
        for acc_list, s in zip(outs, st):
            acc_list.append(s)
        ctx = {'k': cache_attn_k[:, l], 'v': cache_attn_v[:, l], 'ssd': state_ssd[:, l],
               'C': state_mlstm_C[:, l], 'n': state_mlstm_n[:, l], 'm': state_mlstm_m[:, l]}
        y_sample, _ = _trunk_layer(y_sample, mod[1:1 + nb], p, l, ctx)
    return (y_prompt, y_sample) + tuple(jnp.stack(o, axis=1) for o in outs)
```
